```python
import math
import jax, jax.numpy as jnp
from jax import lax
import numpy as np

D_MODEL = 4096
BATCH = 4
SEQ = 2048
DEPTH = 2

D_MIX = D_MODEL // 2
N_BRANCH = 3
HY_WIDTH = D_MIX
HY_ORDER = 2
HY_SHORT = 3
HY_EMB = 33
HY_BANDS = (HY_EMB - 1) // 2
HY_FILTER_HIDDEN = 64
HY_N_INNER = 2
HY_FAST_DECAY = 0.3
HY_SLOW_DECAY = 1.5
HY_DECAY_TARGET = 1e-2
ML_HEADS = 8
ML_DV = D_MIX // ML_HEADS
ML_DK = ML_DV // 2
ML_CHUNK = 64
ML_FGATE_LO = 3.0
ML_FGATE_HI = 6.0
RG_WIDTH = D_MIX
RG_HEADS = 8
RG_BLOCK = RG_WIDTH // RG_HEADS
RG_CONV = 4
RG_C = 8.0
FFN_HIDDEN = ((8 * D_MODEL // 3 + 255) // 256) * 256
EPS = 1e-6
IN_SIZES = (
    (HY_ORDER + 1) * HY_WIDTH,
    ML_HEADS * ML_DK,
    ML_HEADS * ML_DK,
    ML_HEADS * ML_DV,
    ML_HEADS * ML_DV,
    4 * ML_HEADS,
    RG_WIDTH,
    RG_WIDTH,
    N_BRANCH * D_MODEL,
)
D_IN = sum(IN_SIZES)

kernel_name = 'hybrid_hyena_mlstm_rglru_encoder'


def _rms_norm(x, g):
    x32 = x.astype(jnp.float32)
    y = x32 * lax.rsqrt(jnp.mean(x32 * x32, axis=-1, keepdims=True) + EPS)
    return (y * g.astype(jnp.float32)).astype(x.dtype)


def _depthwise_conv(x, w, b, left):
    k = w.shape[0]
    c = x.shape[-1]
    y = lax.conv_general_dilated(
        x, w[:, None, :].astype(x.dtype), window_strides=(1,),
        padding=[(left, k - 1 - left)],
        dimension_numbers=('NWC', 'WIO', 'NWC'), feature_group_count=c)
    return y + b.astype(x.dtype)


def _flip(t):
    return jnp.flip(t, axis=1)


def _split_columns(p):
    offs = []
    acc = 0
    for s in IN_SIZES[:-1]:
        acc += s
        offs.append(acc)
    return jnp.split(p, offs, axis=-1)


def _hyena_filters(seq_len, w1, b1, w2, b2, freq, w3):
    f32 = jnp.float32
    pos = jnp.arange(seq_len, dtype=f32)
    t = pos / (seq_len - 1)
    omega = 2.0 * math.pi * pos / seq_len
    bands = jnp.linspace(1e-4, HY_BANDS - 1, HY_BANDS, dtype=f32)
    ang = omega[:, None] * bands[None, :]
    feat = jnp.concatenate([t[:, None], jnp.cos(ang), -jnp.sin(ang)], axis=-1)
    freq = freq.astype(f32)
    hdn = jnp.sin(freq * (feat @ w1.astype(f32) + b1.astype(f32)))
    for j in range(HY_N_INNER):
        hdn = jnp.sin(freq * (hdn @ w2[j].astype(f32) + b2[j].astype(f32)))
    filt = (hdn @ w3.astype(f32)).reshape(seq_len, HY_ORDER, 2, HY_WIDTH)
    deltas = jnp.abs(jnp.linspace(math.log(HY_DECAY_TARGET) / HY_SLOW_DECAY,
                                  math.log(HY_DECAY_TARGET) / HY_FAST_DECAY,
                                  HY_WIDTH, dtype=f32))
    window = jnp.exp(-t[:, None] * deltas[None, :])
    filt = filt * window[:, None, None, :]
    fwd = filt[:, :, 0]
    bwd = filt[:, :, 1]
    kern = jnp.concatenate(
        [fwd, jnp.zeros((1, HY_ORDER, HY_WIDTH), f32), bwd[:0:-1]], axis=0)
    kern = kern * lax.rsqrt(jnp.sum(kern * kern, axis=0, keepdims=True))
    return jnp.fft.rfft(kern, axis=0)


def _fft_conv(z, kf):
    seq_len = z.shape[1]
    zf = jnp.fft.rfft(z.astype(jnp.float32), n=2 * seq_len, axis=1)
    y = jnp.fft.irfft(zf * kf[None], n=2 * seq_len, axis=1)[:, :seq_len]
    return y.astype(z.dtype)


def _hyena_branch(u, conv_w, conv_b, w1, b1, w2, b2, freq, w3, skip):
    u = _depthwise_conv(u, conv_w, conv_b, (HY_SHORT - 1) // 2)
    x1, x2, v = jnp.split(u, HY_ORDER + 1, axis=-1)
    kf = _hyena_filters(u.shape[1], w1, b1, w2, b2, freq, w3)
    z = x1 * (_fft_conv(v, kf[:, 0]) + skip[0].astype(v.dtype) * v)
    return x2 * (_fft_conv(z, kf[:, 1]) + skip[1].astype(z.dtype) * z)


def _mlstm_chunkwise(q, k, v, i_pre, log_f):
    f32 = jnp.float32
    bsz, seq_len, nh, dk = q.shape
    dv = v.shape[-1]
    nc = seq_len // ML_CHUNK

    def chunks4(t):
        return t.astype(f32).reshape(bsz, nc, ML_CHUNK, nh, t.shape[-1]).transpose(1, 0, 3, 2, 4)

    def chunks3(t):
        return t.astype(f32).reshape(bsz, nc, ML_CHUNK, nh).transpose(1, 0, 3, 2)

    xs = (chunks4(q), chunks4(k), chunks4(v), chunks3(i_pre), chunks3(log_f))
    mask = jnp.tril(jnp.ones((ML_CHUNK, ML_CHUNK), dtype=bool))

    def step(carry, inp):
        c_st, n_st, m_st = carry
        qc, kc, vc, ic, fc = inp
        b = jnp.cumsum(fc, axis=-1)
        g = b[..., -1]
        dmat = b[..., :, None] - b[..., None, :] + ic[..., None, :]
        dmat = jnp.where(mask, dmat, -jnp.inf)
        inter = b + m_st[..., None]
        m_j = jnp.maximum(inter, jnp.max(dmat, axis=-1))
        w_intra = jnp.exp(dmat - m_j[..., None])
        w_inter = jnp.exp(inter - m_j)
        s = jnp.einsum('bhjd,bhld->bhjl', qc, kc) * w_intra
        num = (w_inter[..., None] * jnp.einsum('bhvd,bhjd->bhjv', c_st, qc)
               + jnp.einsum('bhjl,bhlv->bhjv', s, vc))
        den = w_inter * jnp.einsum('bhd,bhjd->bhj', n_st, qc) + jnp.sum(s, axis=-1)
        h = num / jnp.maximum(jnp.abs(den), jnp.exp(-m_j))[..., None]
        lw = g[..., None] - b + ic
        m_new = jnp.maximum(g + m_st, jnp.max(lw, axis=-1))
        wl = jnp.exp(lw - m_new[..., None])
        decay = jnp.exp(g + m_st - m_new)
        c_new = decay[..., None, None] * c_st + jnp.einsum('bhl,bhlv,bhld->bhvd', wl, vc, kc)
        n_new = decay[..., None] * n_st + jnp.einsum('bhl,bhld->bhd', wl, kc)
        return (c_new, n_new, m_new), h

    init = (jnp.zeros((bsz, nh, dv, dk), f32), jnp.zeros((bsz, nh, dk), f32),
            jnp.zeros((bsz, nh), f32))
    _, hs = lax.scan(step, init, xs)
    return hs.transpose(1, 0, 3, 2, 4).reshape(bsz, seq_len, nh, dv)


def _mlstm_branch(q, k, v, o, gates, gate_b, norm_w):
    bsz, seq_len, _ = q.shape
    q = q.reshape(bsz, seq_len, ML_HEADS, ML_DK)
    k = k.reshape(bsz, seq_len, ML_HEADS, ML_DK) * (ML_DK ** -0.5)
    v = v.reshape(bsz, seq_len, ML_HEADS, ML_DV)
    gpre = gates.reshape(bsz, seq_len, 2, 2, ML_HEADS).astype(jnp.float32) + gate_b.astype(jnp.float32)
    i_pre = gpre[:, :, :, 0]
    log_f = jax.nn.log_sigmoid(gpre[:, :, :, 1])
    h_fwd = _mlstm_chunkwise(q, k, v, i_pre[:, :, 0], log_f[:, :, 0])
    h_bwd = _flip(_mlstm_chunkwise(_flip(q), _flip(k), _flip(v),
                                   _flip(i_pre[:, :, 1]), _flip(log_f[:, :, 1])))
    hsum = h_fwd + h_bwd
    hn = hsum * lax.rsqrt(jnp.mean(hsum * hsum, axis=-1, keepdims=True) + EPS)
    hn = hn.reshape(bsz, seq_len, ML_HEADS * ML_DV) * norm_w.astype(jnp.float32)
    return (jax.nn.sigmoid(o.astype(jnp.float32)) * hn).astype(o.dtype)


def _rglru_scan(xc, wa, ba, wx, bx, lam):
    f32 = jnp.float32
    bsz, seq_len, c = xc.shape
    x32 = xc.astype(f32)
    xh = x32.reshape(bsz, seq_len, RG_HEADS, RG_BLOCK)
    r = jax.nn.sigmoid(jnp.einsum('blhi,hij->blhj', xh, wa.astype(f32)).reshape(bsz, seq_len, c)
                       + ba.astype(f32))
    ig = jax.nn.sigmoid(jnp.einsum('blhi,hij->blhj', xh, wx.astype(f32)).reshape(bsz, seq_len, c)
                        + bx.astype(f32))
    log_a = -RG_C * r * jax.nn.softplus(-lam.astype(f32))
    a = jnp.exp(log_a)
    bterm = jnp.sqrt(-jnp.expm1(2.0 * log_a)) * (ig * x32)

    def comb(e1, e2):
        a1, b1 = e1
        a2, b2 = e2
        return a1 * a2, a2 * b1 + b2

    _, h = lax.associative_scan(comb, (a, bterm), axis=1)
    return h


def _rglru_branch(xr, yr, conv_w, conv_b, wa, ba, wx, bx, lam):
    xc = _depthwise_conv(xr, conv_w, conv_b, RG_CONV // 2)
    h_fwd = _rglru_scan(xc, wa[0], ba[0], wx[0], bx[0], lam[0])
    h_bwd = _flip(_rglru_scan(_flip(xc), wa[1], ba[1], wx[1], bx[1], lam[1]))
    return ((h_fwd + h_bwd) * jax.nn.gelu(yr.astype(jnp.float32))).astype(xr.dtype)


def setup_inputs(seed: int = 0) -> dict:
    key = jax.random.key(seed)
    ks = jax.random.split(key, 32)
    f32 = jnp.float32

    def nrm(k, shape, scale):
        return jax.random.normal(k, shape, f32) * scale

    x = nrm(ks[0], (BATCH, SEQ, D_MODEL), 1.0)
    mix_norm = 1.0 + nrm(ks[1], (DEPTH, D_MODEL), 0.05)
    w_in = nrm(ks[2], (DEPTH, D_MODEL, D_IN), D_MODEL ** -0.5)
    b_gate = nrm(ks[3], (DEPTH, N_BRANCH * D_MODEL), 0.01)
    hy_conv_w = nrm(ks[4], (DEPTH, HY_SHORT, (HY_ORDER + 1) * HY_WIDTH), HY_SHORT ** -0.5)
    hy_conv_b = nrm(ks[5], (DEPTH, (HY_ORDER + 1) * HY_WIDTH), 0.01)
    hy_w1 = nrm(ks[6], (DEPTH, HY_EMB, HY_FILTER_HIDDEN), HY_EMB ** -0.5)
    hy_b1 = nrm(ks[7], (DEPTH, HY_FILTER_HIDDEN), 0.01)
    hy_w2 = nrm(ks[8], (DEPTH, HY_N_INNER, HY_FILTER_HIDDEN, HY_FILTER_HIDDEN), HY_FILTER_HIDDEN ** -0.5)
    hy_b2 = nrm(ks[9], (DEPTH, HY_N_INNER, HY_FILTER_HIDDEN), 0.01)
    hy_freq = 1.0 + nrm(ks[10], (DEPTH, HY_FILTER_HIDDEN), 0.05)
    hy_w3 = nrm(ks[11], (DEPTH, HY_FILTER_HIDDEN, HY_ORDER * 2 * HY_WIDTH), HY_FILTER_HIDDEN ** -0.5)
    hy_skip = nrm(ks[12], (DEPTH, HY_ORDER, HY_WIDTH), 0.1)
    ig_b = nrm(ks[13], (DEPTH, 2, 1, ML_HEADS), 0.01)
    fg_b = (jnp.linspace(ML_FGATE_LO, ML_FGATE_HI, ML_HEADS, dtype=f32)[None, None, None, :]
            + nrm(ks[14], (DEPTH, 2, 1, ML_HEADS), 0.01))
    ml_gate_b = jnp.concatenate([ig_b, fg_b], axis=2)
    ml_norm = 1.0 + nrm(ks[15], (DEPTH, ML_HEADS * ML_DV), 0.05)
    rg_conv_w = nrm(ks[16], (DEPTH, RG_CONV, RG_WIDTH), RG_CONV ** -0.5)
    rg_conv_b = nrm(ks[17], (DEPTH, RG_WIDTH), 0.01)
    rg_wa = nrm(ks[18], (DEPTH, 2, RG_HEADS, RG_BLOCK, RG_BLOCK), RG_BLOCK ** -0.5)
    rg_ba = nrm(ks[19], (DEPTH, 2, RG_WIDTH), 0.01)
    rg_wx = nrm(ks[20], (DEPTH, 2, RG_HEADS, RG_BLOCK, RG_BLOCK), RG_BLOCK ** -0.5)
    rg_bx = nrm(ks[21], (DEPTH, 2, RG_WIDTH), 0.01)
    a0 = jax.random.uniform(ks[22], (DEPTH, 2, RG_WIDTH), f32, 0.9, 0.999)
    s0 = a0 ** (1.0 / RG_C)
    rg_lambda = jnp.log(s0) - jnp.log1p(-s0)
    w_br_a = nrm(ks[23], (DEPTH, HY_WIDTH, D_MODEL), HY_WIDTH ** -0.5)
    w_br_b = nrm(ks[24], (DEPTH, ML_HEADS * ML_DV, D_MODEL), (ML_HEADS * ML_DV) ** -0.5)
    w_br_c = nrm(ks[25], (DEPTH, RG_WIDTH, D_MODEL), RG_WIDTH ** -0.5)
    w_out = nrm(ks[26], (DEPTH, D_MODEL, D_MODEL), D_MODEL ** -0.5)
    ffn_norm = 1.0 + nrm(ks[27], (DEPTH, D_MODEL), 0.05)
    w_gate = nrm(ks[28], (DEPTH, D_MODEL, FFN_HIDDEN), D_MODEL ** -0.5)
    w_up = nrm(ks[29], (DEPTH, D_MODEL, FFN_HIDDEN), D_MODEL ** -0.5)
    w_down = nrm(ks[30], (DEPTH, FFN_HIDDEN, D_MODEL), FFN_HIDDEN ** -0.5)
    final_norm = 1.0 + nrm(ks[31], (D_MODEL,), 0.05)
    return {'x': x, 'mix_norm': mix_norm, 'w_in': w_in, 'b_gate': b_gate,
            'hy_conv_w': hy_conv_w, 'hy_conv_b': hy_conv_b, 'hy_w1': hy_w1, 'hy_b1': hy_b1,
            'hy_w2': hy_w2, 'hy_b2': hy_b2, 'hy_freq': hy_freq, 'hy_w3': hy_w3, 'hy_skip': hy_skip,
            'ml_gate_b': ml_gate_b, 'ml_norm': ml_norm,
            'rg_conv_w': rg_conv_w, 'rg_conv_b': rg_conv_b, 'rg_wa': rg_wa, 'rg_ba': rg_ba,
            'rg_wx': rg_wx, 'rg_bx': rg_bx, 'rg_lambda': rg_lambda,
            'w_br_a': w_br_a, 'w_br_b': w_br_b, 'w_br_c': w_br_c, 'w_out': w_out,
            'ffn_norm': ffn_norm, 'w_gate': w_gate, 'w_up': w_up, 'w_down': w_down,
            'final_norm': final_norm}


def reference(x, mix_norm, w_in, b_gate, hy_conv_w, hy_conv_b, hy_w1, hy_b1, hy_w2, hy_b2,
              hy_freq, hy_w3, hy_skip, ml_gate_b, ml_norm, rg_conv_w, rg_conv_b, rg_wa, rg_ba,
              rg_wx, rg_bx, rg_lambda, w_br_a, w_br_b, w_br_c, w_out, ffn_norm, w_gate, w_up,
              w_down, final_norm):
    bsz, seq_len, _ = x.shape
    h = x
    for l in range(DEPTH):
        u = _rms_norm(h, mix_norm[l])
        proj = jnp.einsum('bld,dn->bln', u, w_in[l])
        hy_in, ml_q, ml_k, ml_v, ml_o, ml_g, rg_x, rg_y, g = _split_columns(proj)
        y_a = _hyena_branch(hy_in, hy_conv_w[l], hy_conv_b[l], hy_w1[l], hy_b1[l], hy_w2[l],
                            hy_b2[l], hy_freq[l], hy_w3[l], hy_skip[l])
        y_b = _mlstm_branch(ml_q, ml_k, ml_v, ml_o, ml_g, ml_gate_b[l], ml_norm[l])
        y_c = _rglru_branch(rg_x, rg_y, rg_conv_w[l], rg_conv_b[l], rg_wa[l], rg_ba[l],
                            rg_wx[l], rg_bx[l], rg_lambda[l])
        gate = jax.nn.sigmoid((g + b_gate[l]).reshape(bsz, seq_len, N_BRANCH, D_MODEL))
        merged = (gate[:, :, 0] * jnp.einsum('blc,cd->bld', y_a, w_br_a[l])
                  + gate[:, :, 1] * jnp.einsum('blc,cd->bld', y_b, w_br_b[l])
                  + gate[:, :, 2] * jnp.einsum('blc,cd->bld', y_c, w_br_c[l]))
        h = h + jnp.einsum('bld,de->ble', merged, w_out[l])
        u = _rms_norm(h, ffn_norm[l])
        ff = jax.nn.silu(jnp.einsum('bld,df->blf', u, w_gate[l])) * jnp.einsum('bld,df->blf', u, w_up[l])
        h = h + jnp.einsum('blf,fd->bld', ff, w_down[l])
    return _rms_norm(h, final_norm)
```

```python
import functools
import math

import jax
import jax.numpy as jnp
from jax import lax
from jax.experimental import pallas as pl
from jax.experimental.pallas import tpu as pltpu

F32 = jnp.float32
BF16 = jnp.bfloat16

N_BRANCH = 3
HY_ORDER = 2
HY_SHORT = 3
HY_EMB = 33
HY_BANDS = (HY_EMB - 1) // 2
HY_FILTER_HIDDEN = 64
HY_N_INNER = 2
HY_FAST_DECAY = 0.3
HY_SLOW_DECAY = 1.5
HY_DECAY_TARGET = 1e-2
ML_HEADS = 8
RG_HEADS = 8
RG_CONV = 4
RG_C = 8.0
EPS = 1e-6

V7X_VMEM_LIMIT_BYTES = 56 * 1024 * 1024
LANE = 128
SUBLANE = 8
MM_TILE = 1024
ML_CHUNK = 256
HY_CTILE = 256
HY_ROWS = 512
FFN_PAD = 1024


def _cparams(sem, vmem=V7X_VMEM_LIMIT_BYTES):
    return pltpu.CompilerParams(dimension_semantics=sem, vmem_limit_bytes=vmem)


def _dot(a, b):
    return jnp.dot(a, b, preferred_element_type=F32)


def _dot_hi(a, b):
    return jnp.dot(a, b, preferred_element_type=F32, precision=lax.Precision.HIGHEST)


def _rms_kernel(x_ref, g_ref, o_ref):
    x = x_ref[...]
    ms = jnp.mean(x * x, axis=-1, keepdims=True)
    o_ref[...] = (x * lax.rsqrt(ms + EPS) * g_ref[...]).astype(o_ref.dtype)


def _rmsnorm(x2d, g, out_dtype):
    m, d = x2d.shape
    tm = min(256, m)
    return pl.pallas_call(
        _rms_kernel,
        grid=(m // tm,),
        in_specs=[pl.BlockSpec((tm, d), lambda i: (i, 0)),
                  pl.BlockSpec((1, d), lambda i: (0, 0))],
        out_specs=pl.BlockSpec((tm, d), lambda i: (i, 0)),
        out_shape=jax.ShapeDtypeStruct((m, d), out_dtype),
        compiler_params=_cparams(("parallel",)),
        name="rmsnorm",
    )(x2d, g.reshape(1, d).astype(F32))


def _mm_kernel(a_ref, w_ref, o_ref):
    o_ref[...] = _dot(a_ref[...], w_ref[...]).astype(o_ref.dtype)


def _matmul(a, w, out_dtype, name):
    m, k = a.shape
    n = w.shape[1]
    tm = min(MM_TILE, m)
    tn = min(MM_TILE, n)
    return pl.pallas_call(
        _mm_kernel,
        grid=(m // tm, n // tn),
        in_specs=[pl.BlockSpec((tm, k), lambda i, j: (i, 0)),
                  pl.BlockSpec((k, tn), lambda i, j: (0, j))],
        out_specs=pl.BlockSpec((tm, tn), lambda i, j: (i, j)),
        out_shape=jax.ShapeDtypeStruct((m, n), out_dtype),
        compiler_params=_cparams(("parallel", "parallel")),
        name=name,
    )(a, w)


def _mm_res_kernel(a_ref, w_ref, r_ref, o_ref):
    k = pl.program_id(2)

    @pl.when(k == 0)
    def _():
        o_ref[...] = r_ref[...]

    o_ref[...] += _dot(a_ref[...], w_ref[...])


def _matmul_residual(a, w, res, tk, name):
    m, k = a.shape
    n = w.shape[1]
    tm = min(MM_TILE, m)
    tn = min(MM_TILE, n)
    return pl.pallas_call(
        _mm_res_kernel,
        grid=(m // tm, n // tn, k // tk),
        in_specs=[pl.BlockSpec((tm, tk), lambda i, j, kk: (i, kk)),
                  pl.BlockSpec((tk, tn), lambda i, j, kk: (kk, j)),
                  pl.BlockSpec((tm, tn), lambda i, j, kk: (i, j))],
        out_specs=pl.BlockSpec((tm, tn), lambda i, j, kk: (i, j)),
        out_shape=jax.ShapeDtypeStruct((m, n), F32),
        compiler_params=_cparams(("parallel", "parallel", "arbitrary")),
        name=name,
    )(a, w, res)


def _ffn_gu_kernel(a_ref, wg_ref, wu_ref, o_ref):
    a = a_ref[...]
    g = _dot(a, wg_ref[...])
    u = _dot(a, wu_ref[...])
    o_ref[...] = (g * jax.nn.sigmoid(g) * u).astype(o_ref.dtype)


def _ffn_gate_up(a, wg, wu):
    m, k = a.shape
    n = wg.shape[1]
    tm = min(MM_TILE, m)
    tn = MM_TILE // 2
    return pl.pallas_call(
        _ffn_gu_kernel,
        grid=(m // tm, n // tn),
        in_specs=[pl.BlockSpec((tm, k), lambda i, j: (i, 0)),
                  pl.BlockSpec((k, tn), lambda i, j: (0, j)),
                  pl.BlockSpec((k, tn), lambda i, j: (0, j))],
        out_specs=pl.BlockSpec((tm, tn), lambda i, j: (i, j)),
        out_shape=jax.ShapeDtypeStruct((m, n), BF16),
        compiler_params=_cparams(("parallel", "parallel")),
        name="ffn_gate_up",
    )(a, wg, wu)


def _merge_kernel(ya_ref, yb_ref, yc_ref, wa_ref, wb_ref, wc_ref,
                  ga_ref, gb_ref, gc_ref, ba_ref, bb_ref, bc_ref, o_ref):
    acc = jax.nn.sigmoid(ga_ref[...] + ba_ref[...]) * _dot(ya_ref[...], wa_ref[...])
    acc += jax.nn.sigmoid(gb_ref[...] + bb_ref[...]) * _dot(yb_ref[...], wb_ref[...])
    acc += jax.nn.sigmoid(gc_ref[...] + bc_ref[...]) * _dot(yc_ref[...], wc_ref[...])
    o_ref[...] = acc.astype(o_ref.dtype)


def _merge(ya, yb, yc, wa, wb, wc, g, bg):
    m, c = ya.shape
    d = wa.shape[1]
    tm = min(MM_TILE // 2, m)
    tn = MM_TILE // 2
    nb = d // tn
    yspec = pl.BlockSpec((tm, c), lambda i, j: (i, 0))
    wspec = pl.BlockSpec((c, tn), lambda i, j: (0, j))

    def gspec(br):
        return pl.BlockSpec((tm, tn), lambda i, j: (i, br * nb + j))

    def bspec(br):
        return pl.BlockSpec((1, tn), lambda i, j: (0, br * nb + j))

    return pl.pallas_call(
        _merge_kernel,
        grid=(m // tm, nb),
        in_specs=[yspec, yspec, yspec, wspec, wspec, wspec,
                  gspec(0), gspec(1), gspec(2), bspec(0), bspec(1), bspec(2)],
        out_specs=pl.BlockSpec((tm, tn), lambda i, j: (i, j)),
        out_shape=jax.ShapeDtypeStruct((m, d), BF16),
        compiler_params=_cparams(("parallel", "parallel")),
        name="branch_merge",
    )(ya, yb, yc, wa, wb, wc, g, g, g, bg, bg, bg)


def _dft_tables(seq_len):
    idx = jnp.arange(seq_len, dtype=jnp.int32)
    prod = (idx[:, None] * idx[None, :]) % (2 * seq_len)
    ang = prod.astype(F32) * (math.pi / seq_len)
    return jnp.cos(ang).astype(BF16), jnp.sin(ang).astype(BF16)


def _filter_features(seq_len):
    pos = jnp.arange(seq_len, dtype=F32)
    t = pos / (seq_len - 1)
    omega = 2.0 * math.pi * pos / seq_len
    bands = jnp.linspace(1e-4, HY_BANDS - 1, HY_BANDS, dtype=F32)
    ang = omega[:, None] * bands[None, :]
    feat = jnp.concatenate([t[:, None], jnp.cos(ang), -jnp.sin(ang)], axis=-1)
    return t, feat


def _filter_mlp_kernel(feat_ref, w1_ref, b1_ref, w2_ref, b2_ref, freq_ref, o_ref):
    freq = freq_ref[...]
    hdn = jnp.sin(freq * (_dot_hi(feat_ref[...], w1_ref[...]) + b1_ref[...]))
    for j in range(HY_N_INNER):
        hdn = jnp.sin(freq * (_dot_hi(hdn, w2_ref[j]) + b2_ref[j]))
    o_ref[...] = hdn


def _filter_mlp(feat_pad, w1_pad, b1, w2, b2, freq):
    seq_len = feat_pad.shape[0]
    hid = w1_pad.shape[1]
    return pl.pallas_call(
        _filter_mlp_kernel,
        out_shape=jax.ShapeDtypeStruct((seq_len, hid), F32),
        name="hyena_filter_mlp",
    )(feat_pad, w1_pad, b1.reshape(1, hid), w2, b2.reshape(HY_N_INNER, 1, hid), freq.reshape(1, hid))


def _filter_spec_kernel(hdn_ref, wf_ref, wb_ref, t_ref, dl_ref, cm_ref, sm_ref,
                        kre_ref, kim_ref, kl_ref):
    seq_len = hdn_ref.shape[0]
    hdn = hdn_ref[...]
    win = jnp.exp(-t_ref[...] * dl_ref[...])
    row = lax.broadcasted_iota(jnp.int32, (seq_len, 1), 0)
    fwd = _dot_hi(hdn, wf_ref[...]) * win
    bwd = jnp.where(row == 0, 0.0, _dot_hi(hdn, wb_ref[...]) * win)
    nrm = lax.rsqrt(jnp.sum(fwd * fwd, axis=0, keepdims=True)
                    + jnp.sum(bwd * bwd, axis=0, keepdims=True))
    even = (fwd + bwd) * nrm
    odd = (bwd - fwd) * nrm
    sgn = (1 - 2 * (row & 1)).astype(F32)
    wfreq = jnp.where(row == 0, 0.5 / seq_len, 1.0 / seq_len)
    kre_ref[0] = (_dot(cm_ref[...], even.astype(BF16)) * wfreq).astype(kre_ref.dtype)
    kim_ref[0] = (_dot(sm_ref[...], odd.astype(BF16)) * wfreq).astype(kim_ref.dtype)
    kl_ref[0] = jnp.sum(sgn * even, axis=0, keepdims=True) * (0.5 / seq_len)


def _filter_spectra(hdn, w3, t_col, deltas, cm, sm):
    seq_len, hid = hdn.shape
    c = deltas.shape[1]
    tc = HY_CTILE
    nct = c // tc
    const2 = lambda o, j: (0, 0)
    return pl.pallas_call(
        _filter_spec_kernel,
        grid=(HY_ORDER, nct),
        in_specs=[pl.BlockSpec((seq_len, hid), const2),
                  pl.BlockSpec((hid, tc), lambda o, j: (0, o * 2 * nct + j)),
                  pl.BlockSpec((hid, tc), lambda o, j: (0, o * 2 * nct + nct + j)),
                  pl.BlockSpec((seq_len, 1), const2),
                  pl.BlockSpec((1, tc), lambda o, j: (0, j)),
                  pl.BlockSpec((seq_len, seq_len), const2, pipeline_mode=pl.Buffered(1)),
                  pl.BlockSpec((seq_len, seq_len), const2, pipeline_mode=pl.Buffered(1))],
        out_specs=[pl.BlockSpec((1, seq_len, tc), lambda o, j: (o, 0, j)),
                   pl.BlockSpec((1, seq_len, tc), lambda o, j: (o, 0, j)),
                   pl.BlockSpec((1, 1, tc), lambda o, j: (o, 0, j))],
        out_shape=[jax.ShapeDtypeStruct((HY_ORDER, seq_len, c), BF16),
                   jax.ShapeDtypeStruct((HY_ORDER, seq_len, c), BF16),
                   jax.ShapeDtypeStruct((HY_ORDER, 1, c), F32)],
        compiler_params=_cparams(("parallel", "parallel")),
        name="hyena_filter_spectra",
    )(hdn, w3, w3, t_col, deltas, cm, sm)


def _short_conv(x, w, b, seq_len):
    row = lax.broadcasted_iota(jnp.int32, (seq_len, 1), 0)
    prev = jnp.where(row == 0, 0.0, pltpu.roll(x, 1, 0))
    nxt = jnp.where(row == seq_len - 1, 0.0, pltpu.roll(x, seq_len - 1, 0))
    return prev * w[0:1] + x * w[1:2] + nxt * w[2:3] + b


def _hyena_kernel(x1_ref, x2_ref, v_ref, w1_ref, w2_ref, wv_ref, b1_ref, b2_ref, bv_ref,
                  skip_ref, kre_ref, kim_ref, kl_ref, cm_ref, sm_ref, o_ref,
                  sig_ref, gate_ref, u_ref, p_ref):
    seq_len = v_ref.shape[1]
    nchunk = seq_len // HY_ROWS
    row = lax.broadcasted_iota(jnp.int32, (seq_len, 1), 0)
    sgn = (1 - 2 * (row & 1)).astype(F32)

    def long_conv(order, epilogue):
        sig = sig_ref[...]
        sig_b = sig.astype(BF16)
        nyq = jnp.sum(sig * sgn, axis=0, keepdims=True) * kl_ref[order]
        for ch in range(nchunk):
            rows = slice(ch * HY_ROWS, (ch + 1) * HY_ROWS)
            re = _dot(cm_ref[rows, :], sig_b)
            im = _dot(sm_ref[rows, :], sig_b)
            kre = kre_ref[order, rows, :].astype(F32)
            kim = kim_ref[order, rows, :].astype(F32)
            u_ref[rows, :] = (re * kre + im * kim).astype(BF16)
            p_ref[rows, :] = (im * kre - re * kim).astype(BF16)
        for ch in range(nchunk):
            rows = slice(ch * HY_ROWS, (ch + 1) * HY_ROWS)
            y = _dot(cm_ref[rows, :], u_ref[...]) + _dot(sm_ref[rows, :], p_ref[...])
            epilogue(rows, y + sgn[rows] * nyq)

    sig_ref[...] = _short_conv(v_ref[0], wv_ref[...], bv_ref[...], seq_len)
    gate_ref[...] = _short_conv(x1_ref[0], w1_ref[...], b1_ref[...], seq_len)

    def first(rows, y):
        sig = sig_ref[rows, :]
        sig_ref[rows, :] = gate_ref[rows, :] * (y + skip_ref[0:1, :] * sig)

    long_conv(0, first)
    gate_ref[...] = _short_conv(x2_ref[0], w2_ref[...], b2_ref[...], seq_len)

    def second(rows, y):
        o_ref[0, rows, :] = (gate_ref[rows, :] * (y + skip_ref[1:2, :] * sig_ref[rows, :])).astype(o_ref.dtype)

    long_conv(1, second)


def _hyena(hy, conv_w, conv_b, skip, kre, kim, kl, cm, sm):
    bsz, seq_len, c3 = hy.shape
    c = c3 // (HY_ORDER + 1)
    tc = HY_CTILE
    nct = c // tc

    def xspec(part):
        return pl.BlockSpec((1, seq_len, tc), lambda j, b: (b, 0, part * nct + j))

    def wspec(part):
        return pl.BlockSpec((HY_SHORT, tc), lambda j, b: (0, part * nct + j))

    def bspec(part):
        return pl.BlockSpec((1, tc), lambda j, b: (0, part * nct + j))

    const2 = lambda j, b: (0, 0)
    return pl.pallas_call(
        _hyena_kernel,
        grid=(nct, bsz),
        in_specs=[xspec(0), xspec(1), xspec(2), wspec(0), wspec(1), wspec(2),
                  bspec(0), bspec(1), bspec(2),
                  pl.BlockSpec((HY_ORDER, tc), lambda j, b: (0, j)),
                  pl.BlockSpec((HY_ORDER, seq_len, tc), lambda j, b: (0, 0, j)),
                  pl.BlockSpec((HY_ORDER, seq_len, tc), lambda j, b: (0, 0, j)),
                  pl.BlockSpec((HY_ORDER, 1, tc), lambda j, b: (0, 0, j)),
                  pl.BlockSpec((seq_len, seq_len), const2, pipeline_mode=pl.Buffered(1)),
                  pl.BlockSpec((seq_len, seq_len), const2, pipeline_mode=pl.Buffered(1))],
        out_specs=pl.BlockSpec((1, seq_len, tc), lambda j, b: (b, 0, j)),
        out_shape=jax.ShapeDtypeStruct((bsz, seq_len, c), BF16),
        scratch_shapes=[pltpu.VMEM((seq_len, tc), F32), pltpu.VMEM((seq_len, tc), F32),
                        pltpu.VMEM((seq_len, tc), BF16), pltpu.VMEM((seq_len, tc), BF16)],
        compiler_params=_cparams(("parallel", "arbitrary")),
        name="hyena_conv",
    )(hy, hy, hy, conv_w, conv_w, conv_w, conv_b, conv_b, conv_b, skip, kre, kim, kl, cm, sm)


def _mlstm_kernel(q_ref, k_ref, v_ref, og_ref, kt_ref, gc_ref, gr_ref, bc_ref, br_ref, nw_ref,
                  o_ref, hf_ref, hb_ref, *, chunk, nchunks, dk, dv):
    rowi = lax.broadcasted_iota(jnp.int32, (chunk, chunk), 0)
    coli = lax.broadcasted_iota(jnp.int32, (chunk, chunk), 1)
    lower = coli <= rowi
    upper = coli >= rowi
    bias_c = bc_ref[0]
    bias_r = br_ref[0]
    scale = dk ** -0.5
    neg_inf = -jnp.inf

    def step(c, direction, state):
        c_st, n_st, m_st = state
        r0 = pl.multiple_of(c * chunk, chunk)
        gcol = gc_ref[0, 0, pl.ds(r0, chunk), :] + bias_c
        grow = gr_ref[0, 0, c] + bias_r
        gi, gf = 2 * direction, 2 * direction + 1
        i_col = gcol[:, gi:gi + 1]
        f_col = jax.nn.log_sigmoid(gcol[:, gf:gf + 1])
        i_row = grow[gi:gi + 1, :]
        f_row = jax.nn.log_sigmoid(grow[gf:gf + 1, :])
        mask, mask_t = (lower, upper) if direction == 0 else (upper, lower)
        b_col = jnp.sum(jnp.where(mask, f_row, 0.0), axis=1, keepdims=True)
        b_row = jnp.sum(jnp.where(mask_t, f_col, 0.0), axis=0, keepdims=True)
        g = jnp.sum(f_row, axis=1, keepdims=True)
        dmat = jnp.where(mask, b_col - b_row + i_row, neg_inf)
        inter = b_col + m_st
        m_j = jnp.maximum(inter, jnp.max(dmat, axis=1, keepdims=True))
        w_intra = jnp.exp(dmat - m_j)
        w_inter = jnp.exp(inter - m_j)
        qc = q_ref[0, pl.ds(r0, chunk), :] * scale
        qb = qc.astype(BF16)
        kt = kt_ref[0, 0, c].astype(BF16)
        kc = k_ref[0, pl.ds(r0, chunk), :]
        vc = v_ref[0, pl.ds(r0, chunk), :]
        s = _dot(qb, kt) * w_intra
        num = w_inter * _dot(qb, c_st.astype(BF16)) + _dot(s.astype(BF16), vc.astype(BF16))
        den = (w_inter * jnp.sum(qc * n_st, axis=1, keepdims=True)
               + jnp.sum(s, axis=1, keepdims=True))
        h = num / jnp.maximum(jnp.abs(den), jnp.exp(-m_j))
        lw = g - b_col + i_col
        m_new = jnp.maximum(g + m_st, jnp.max(lw, axis=0, keepdims=True))
        wl = jnp.exp(lw - m_new)
        decay = jnp.exp(g + m_st - m_new)
        c_new = decay * c_st + _dot(kt, (wl * vc).astype(BF16))
        n_new = decay * n_st + jnp.sum(wl * kc, axis=0, keepdims=True)
        return h, (c_new, n_new, m_new)

    def body(c, carry):
        st_f, st_b = carry
        h_f, st_f = step(c, 0, st_f)
        hf_ref[pl.ds(pl.multiple_of(c * chunk, chunk), chunk), :] = h_f
        cb = nchunks - 1 - c
        h_b, st_b = step(cb, 1, st_b)
        hb_ref[pl.ds(pl.multiple_of(cb * chunk, chunk), chunk), :] = h_b
        return st_f, st_b

    zero = (jnp.zeros((dk, dv), F32), jnp.zeros((1, dk), F32), jnp.zeros((1, 1), F32))
    lax.fori_loop(0, nchunks, body, (zero, zero))

    hsum = hf_ref[...] + hb_ref[...]
    hn = hsum * lax.rsqrt(jnp.mean(hsum * hsum, axis=-1, keepdims=True) + EPS) * nw_ref[...]
    o_ref[0] = (jax.nn.sigmoid(og_ref[0]) * hn).astype(o_ref.dtype)


def _mlstm(ml, gates, gate_b, norm_w):
    bsz, seq_len, _ = ml.shape
    nh = ML_HEADS
    dv = norm_w.shape[0] // nh
    dk = dv // 2
    chunk = min(ML_CHUNK, seq_len)
    nchunks = seq_len // chunk
    koff = nh * dk
    k_t = (ml[:, :, koff:2 * koff].reshape(bsz, nchunks, chunk, nh, dk)
           .transpose(0, 3, 1, 4, 2))
    g4 = gates.reshape(bsz, seq_len, 4, nh)
    g_col = g4.transpose(0, 3, 1, 2)
    g_row = g4.reshape(bsz, nchunks, chunk, 4, nh).transpose(0, 4, 1, 3, 2)
    gb = gate_b.astype(F32).reshape(4, nh)
    b_col = gb.T.reshape(nh, 1, 4)
    b_row = gb.T.reshape(nh, 4, 1)
    kern = functools.partial(_mlstm_kernel, chunk=chunk, nchunks=nchunks, dk=dk, dv=dv)
    return pl.pallas_call(
        kern,
        grid=(bsz, nh),
        in_specs=[pl.BlockSpec((1, seq_len, dk), lambda b, h: (b, 0, h)),
                  pl.BlockSpec((1, seq_len, dk), lambda b, h: (b, 0, nh + h)),
                  pl.BlockSpec((1, seq_len, dv), lambda b, h: (b, 0, nh + h)),
                  pl.BlockSpec((1, seq_len, dv), lambda b, h: (b, 0, 2 * nh + h)),
                  pl.BlockSpec((1, 1, nchunks, dk, chunk), lambda b, h: (b, h, 0, 0, 0)),
                  pl.BlockSpec((1, 1, seq_len, 4), lambda b, h: (b, h, 0, 0)),
                  pl.BlockSpec((1, 1, nchunks, 4, chunk), lambda b, h: (b, h, 0, 0, 0)),
                  pl.BlockSpec((1, 1, 4), lambda b, h: (h, 0, 0)),
                  pl.BlockSpec((1, 4, 1), lambda b, h: (h, 0, 0)),
                  pl.BlockSpec((1, dv), lambda b, h: (0, h))],
        out_specs=pl.BlockSpec((1, seq_len, dv), lambda b, h: (b, 0, h)),
        out_shape=jax.ShapeDtypeStruct((bsz, seq_len, nh * dv), BF16),
        scratch_shapes=[pltpu.VMEM((seq_len, dv), F32), pltpu.VMEM((seq_len, dv), F32)],
        compiler_params=_cparams(("parallel", "parallel")),
        name="mlstm",
    )(ml, ml, ml, ml, k_t, g_col, g_row, b_col, b_row, norm_w.reshape(1, nh * dv).astype(F32))


def _rglru_kernel(x_ref, y_ref, cw_ref, cb_ref, wa_ref, wx_ref, ba_ref, bx_ref, lam_ref,
                  o_ref, a_ref, b_ref, h_ref):
    seq_len, width = x_ref.shape[1], x_ref.shape[2]
    nblk = seq_len // SUBLANE
    x = x_ref[0]
    row = lax.broadcasted_iota(jnp.int32, (seq_len, 1), 0)
    cw = cw_ref[...]
    xc = (jnp.where(row < 2, 0.0, pltpu.roll(x, 2, 0)) * cw[0:1]
          + jnp.where(row < 1, 0.0, pltpu.roll(x, 1, 0)) * cw[1:2]
          + x * cw[2:3]
          + jnp.where(row == seq_len - 1, 0.0, pltpu.roll(x, seq_len - 1, 0)) * cw[3:4]
          + cb_ref[...])
    xb = xc.astype(BF16)
    row8 = lax.broadcasted_iota(jnp.int32, (SUBLANE, width), 0)

    for direction in range(2):
        r = jax.nn.sigmoid(_dot(xb, wa_ref[direction, 0]) + ba_ref[direction:direction + 1, :])
        ig = jax.nn.sigmoid(_dot(xb, wx_ref[direction, 0]) + bx_ref[direction:direction + 1, :])
        log_a = -RG_C * r * jax.nn.softplus(-lam_ref[direction:direction + 1, :])
        a_ref[...] = jnp.exp(log_a)
        th = jnp.tanh(log_a)
        b_ref[...] = jnp.sqrt(-2.0 * th / (1.0 - th)) * (ig * xc)

        def block(i, carry, direction=direction):
            blk = i if direction == 0 else nblk - 1 - i
            r0 = pl.multiple_of(blk * SUBLANE, SUBLANE)
            a = a_ref[pl.ds(r0, SUBLANE), :]
            b = b_ref[pl.ds(r0, SUBLANE), :]
            for d in (1, 2, 4):
                if direction == 0:
                    shift, valid = d, row8 >= d
                else:
                    shift, valid = SUBLANE - d, row8 < SUBLANE - d
                a_s = jnp.where(valid, pltpu.roll(a, shift, 0), 1.0)
                b_s = jnp.where(valid, pltpu.roll(b, shift, 0), 0.0)
                b = a * b_s + b
                a = a * a_s
            h = b + a * carry
            if direction == 0:
                h_ref[pl.ds(r0, SUBLANE), :] = h
                return h[SUBLANE - 1:SUBLANE, :]
            h_ref[pl.ds(r0, SUBLANE), :] += h
            return h[0:1, :]

        lax.fori_loop(0, nblk, block, jnp.zeros((1, width), F32), unroll=4)

    o_ref[0] = (h_ref[...] * jax.nn.gelu(y_ref[0])).astype(o_ref.dtype)


def _rglru(rg, conv_w, conv_b, wa, ba, wx, bx, lam):
    bsz, seq_len, w2 = rg.shape
    width = w2 // 2
    nh = RG_HEADS
    blk = width // nh
    wspec = pl.BlockSpec((2, 1, blk, blk), lambda b, h: (0, h, 0, 0))
    vspec = pl.BlockSpec((2, blk), lambda b, h: (0, h))
    return pl.pallas_call(
        _rglru_kernel,
        grid=(bsz, nh),
        in_specs=[pl.BlockSpec((1, seq_len, blk), lambda b, h: (b, 0, h)),
                  pl.BlockSpec((1, seq_len, blk), lambda b, h: (b, 0, nh + h)),
                  pl.BlockSpec((RG_CONV, blk), lambda b, h: (0, h)),
                  pl.BlockSpec((1, blk), lambda b, h: (0, h)),
                  wspec, wspec, vspec, vspec, vspec],
        out_specs=pl.BlockSpec((1, seq_len, blk), lambda b, h: (b, 0, h)),
        out_shape=jax.ShapeDtypeStruct((bsz, seq_len, width), BF16),
        scratch_shapes=[pltpu.VMEM((seq_len, blk), F32)] * 3,
        compiler_params=_cparams(("parallel", "parallel")),
        name="rglru",
    )(rg, rg, conv_w.astype(F32), conv_b.reshape(1, width).astype(F32),
      wa.astype(BF16), wx.astype(BF16), ba.astype(F32), bx.astype(F32), lam.astype(F32))


def _pad_cols(w, n):
    return jnp.pad(w, ((0, 0), (0, n - w.shape[1])))


def kernel(x, mix_norm, w_in, b_gate, hy_conv_w, hy_conv_b, hy_w1, hy_b1, hy_w2, hy_b2, hy_freq, hy_w3, hy_skip, ml_gate_b, ml_norm, rg_conv_w, rg_conv_b, rg_wa, rg_ba, rg_wx, rg_bx, rg_lambda, w_br_a, w_br_b, w_br_c, w_out, ffn_norm, w_gate, w_up, w_down, final_norm):
    bsz, seq_len, d_model = x.shape
    depth = mix_norm.shape[0]
    m = bsz * seq_len
    d_mix = d_model // 2
    dv = d_mix // ML_HEADS
    dk = dv // 2
    n_hy = (HY_ORDER + 1) * d_mix
    n_ml = 2 * ML_HEADS * dk + 2 * ML_HEADS * dv
    n_gt = 4 * ML_HEADS
    n_rg = 2 * d_mix
    o_ml = n_hy
    o_gt = o_ml + n_ml
    o_rg = o_gt + n_gt
    o_g = o_rg + n_rg
    ffn_hidden = w_gate.shape[2]
    ffn_pad = -(-ffn_hidden // FFN_PAD) * FFN_PAD

    cm, sm = _dft_tables(seq_len)
    t_pos, feat = _filter_features(seq_len)
    feat_pad = _pad_cols(feat, LANE)
    deltas = jnp.abs(jnp.linspace(math.log(HY_DECAY_TARGET) / HY_SLOW_DECAY,
                                  math.log(HY_DECAY_TARGET) / HY_FAST_DECAY,
                                  d_mix, dtype=F32)).reshape(1, d_mix)

    h = x.reshape(m, d_model).astype(F32)
    for l in range(depth):
        w = w_in[l]
        u = _rmsnorm(h, mix_norm[l], BF16)
        hy = _matmul(u, w[:, :o_ml].astype(BF16), F32, "in_proj_hyena")
        ml = _matmul(u, w[:, o_ml:o_gt].astype(BF16), F32, "in_proj_mlstm")
        gt = _matmul(u, _pad_cols(w[:, o_gt:o_rg], LANE).astype(BF16), F32, "in_proj_mlstm_gates")[:, :n_gt]
        rg = _matmul(u, w[:, o_rg:o_g].astype(BF16), F32, "in_proj_rglru")
        g = _matmul(u, w[:, o_g:].astype(BF16), F32, "in_proj_merge_gates")

        w1_pad = jnp.pad(hy_w1[l].astype(F32), ((0, LANE - HY_EMB), (0, 0)))
        hdn = _filter_mlp(feat_pad, w1_pad, hy_b1[l].astype(F32), hy_w2[l].astype(F32),
                          hy_b2[l].astype(F32), hy_freq[l].astype(F32))
        kre, kim, kl = _filter_spectra(hdn, hy_w3[l].astype(F32), t_pos.reshape(seq_len, 1), deltas, cm, sm)
        y_a = _hyena(hy.reshape(bsz, seq_len, n_hy), hy_conv_w[l].astype(F32),
                     hy_conv_b[l].reshape(1, n_hy).astype(F32), hy_skip[l].astype(F32),
                     kre, kim, kl, cm, sm)

        y_b = _mlstm(ml.reshape(bsz, seq_len, n_ml), gt.reshape(bsz, seq_len, n_gt),
                     ml_gate_b[l], ml_norm[l])
        y_c = _rglru(rg.reshape(bsz, seq_len, n_rg), rg_conv_w[l], rg_conv_b[l],
                     rg_wa[l], rg_ba[l], rg_wx[l], rg_bx[l], rg_lambda[l])

        merged = _merge(y_a.reshape(m, d_mix), y_b.reshape(m, d_mix), y_c.reshape(m, d_mix),
                        w_br_a[l].astype(BF16), w_br_b[l].astype(BF16), w_br_c[l].astype(BF16),
                        g, b_gate[l].reshape(1, N_BRANCH * d_model).astype(F32))
        h = _matmul_residual(merged, w_out[l].astype(BF16), h, d_model, "out_proj")

        u = _rmsnorm(h, ffn_norm[l], BF16)
        ff = _ffn_gate_up(u, _pad_cols(w_gate[l], ffn_pad).astype(BF16),
                          _pad_cols(w_up[l], ffn_pad).astype(BF16))
        w_dn = jnp.pad(w_down[l], ((0, ffn_pad - ffn_hidden), (0, 0))).astype(BF16)
        h = _matmul_residual(ff, w_dn, h, ffn_pad // 4, "ffn_down")

    out = _rmsnorm(h, final_norm, x.dtype)
    return out.reshape(bsz, seq_len, d_model)
```

```python
import functools
import math

import jax
import jax.numpy as jnp
from jax import lax
from jax.experimental import pallas as pl
from jax.experimental.pallas import tpu as pltpu

F32 = jnp.float32
BF16 = jnp.bfloat16

N_BRANCH = 3
HY_ORDER = 2
HY_SHORT = 3
HY_EMB = 33
HY_BANDS = (HY_EMB - 1) // 2
HY_FILTER_HIDDEN = 64
HY_N_INNER = 2
HY_FAST_DECAY = 0.3
HY_SLOW_DECAY = 1.5
HY_DECAY_TARGET = 1e-2
ML_HEADS = 8
RG_HEADS = 8
RG_CONV = 4
RG_C = 8.0
EPS = 1e-6

V7X_VMEM_LIMIT_BYTES = 56 * 1024 * 1024
LANE = 128
SUBLANE = 8
MM_TILE = 1024
ML_CHUNK = 256
HY_CTILE = 256
HY_ROWS = 512
FFN_PAD = 1024
CAST_BLOCK_ELEMS = 1024 * 1024
DFT_SPLIT = 64


def _cparams(sem, vmem=V7X_VMEM_LIMIT_BYTES):
    return pltpu.CompilerParams(dimension_semantics=sem, vmem_limit_bytes=vmem)


def _dot(a, b):
    return jnp.dot(a, b, preferred_element_type=F32)


def _dot_hi(a, b):
    return jnp.dot(a, b, preferred_element_type=F32, precision=lax.Precision.HIGHEST)


def _cast_kernel(x_ref, o_ref, *, src_row_blocks, src_col_blocks):
    inside = (pl.program_id(0) < src_row_blocks) & (pl.program_id(1) < src_col_blocks)
    o_ref[...] = jnp.where(inside, x_ref[...], 0.0).astype(o_ref.dtype)


def _cast_weight(w_stack, layer, col0=0, ncols=None, out_rows=None, out_cols=None):
    _, k, n = w_stack.shape
    ncols = n - col0 if ncols is None else ncols
    out_rows = k if out_rows is None else out_rows
    out_cols = ncols if out_cols is None else out_cols
    tc = next(t for t in (4096, 2048, 1024, 512, 256, LANE)
              if ncols % t == 0 and out_cols % t == 0 and col0 % t == 0)
    tr = next(t for t in (2048, 1024, 512, 256)
              if k % t == 0 and out_rows % t == 0 and t * tc <= CAST_BLOCK_ELEMS)
    src_rb, src_cb, cb0 = k // tr, ncols // tc, col0 // tc
    kern = functools.partial(_cast_kernel, src_row_blocks=src_rb, src_col_blocks=src_cb)
    return pl.pallas_call(
        kern,
        grid=(out_rows // tr, out_cols // tc),
        in_specs=[pl.BlockSpec((None, tr, tc), lambda i, j: (layer, jnp.minimum(i, src_rb - 1),
                                                               cb0 + jnp.minimum(j, src_cb - 1)))],
        out_specs=pl.BlockSpec((tr, tc), lambda i, j: (i, j)),
        out_shape=jax.ShapeDtypeStruct((out_rows, out_cols), BF16),
        compiler_params=_cparams(("parallel", "parallel")),
        name="weight_cast",
    )(w_stack)


def _cast_t_kernel(a_ref, o_ref):
    o_ref[...] = a_ref[...].T.astype(o_ref.dtype)


def _cast_t_shift_kernel(a_ref, b_ref, o_ref, *, shift):
    rows = jnp.concatenate([a_ref[shift:, :], b_ref[...]], axis=0)
    o_ref[...] = rows.T.astype(o_ref.dtype)


def _cast_weight_t(w_t, layer, row0, nrows):
    _, _, k = w_t.shape
    tr = min(CAST_BLOCK_ELEMS // k, nrows)
    shift = row0 % tr
    base = row0 - shift
    out_spec = pl.BlockSpec((k, tr), lambda i: (0, i))
    a_spec = pl.BlockSpec((None, tr, k), lambda i: (layer, base // tr + i, 0))
    common = dict(grid=(nrows // tr,), out_specs=out_spec,
                  out_shape=jax.ShapeDtypeStruct((k, nrows), BF16),
                  compiler_params=_cparams(("parallel",)))
    if shift == 0:
        return pl.pallas_call(_cast_t_kernel, in_specs=[a_spec], name="weight_cast_t", **common)(w_t)
    b_spec = pl.BlockSpec((None, shift, k), lambda i: (layer, (base + (i + 1) * tr) // shift, 0))
    kern = functools.partial(_cast_t_shift_kernel, shift=shift)
    return pl.pallas_call(kern, in_specs=[a_spec, b_spec], name="weight_cast_t_shifted", **common)(w_t, w_t)


def _rms_kernel(x_ref, g_ref, o_ref):
    x = x_ref[...]
    ms = jnp.mean(x * x, axis=-1, keepdims=True)
    o_ref[...] = (x * lax.rsqrt(ms + EPS) * g_ref[...]).astype(o_ref.dtype)


def _rmsnorm(x2d, g, out_dtype):
    m, d = x2d.shape
    tm = min(256, m)
    return pl.pallas_call(
        _rms_kernel,
        grid=(m // tm,),
        in_specs=[pl.BlockSpec((tm, d), lambda i: (i, 0)),
                  pl.BlockSpec((1, d), lambda i: (0, 0))],
        out_specs=pl.BlockSpec((tm, d), lambda i: (i, 0)),
        out_shape=jax.ShapeDtypeStruct((m, d), out_dtype),
        compiler_params=_cparams(("parallel",)),
        name="rmsnorm",
    )(x2d, g.reshape(1, d).astype(F32))


def _mm_kernel(a_ref, w_ref, o_ref):
    o_ref[...] = _dot(a_ref[...], w_ref[...]).astype(o_ref.dtype)


def _matmul(a, w, out_dtype, name, col0=0, ncols=None):
    m, k = a.shape
    n = w.shape[1] - col0 if ncols is None else ncols
    tm = min(MM_TILE, m)
    tn = min(MM_TILE, n)
    cb0 = col0 // tn
    return pl.pallas_call(
        _mm_kernel,
        grid=(m // tm, n // tn),
        in_specs=[pl.BlockSpec((tm, k), lambda i, j: (i, 0)),
                  pl.BlockSpec((k, tn), lambda i, j: (0, cb0 + j))],
        out_specs=pl.BlockSpec((tm, tn), lambda i, j: (i, j)),
        out_shape=jax.ShapeDtypeStruct((m, n), out_dtype),
        compiler_params=_cparams(("parallel", "parallel")),
        name=name,
    )(a, w)


def _mm_sigmoid_kernel(a_ref, w_ref, b_ref, o_ref):
    o_ref[...] = jax.nn.sigmoid(_dot(a_ref[...], w_ref[...]) + b_ref[...]).astype(o_ref.dtype)


def _matmul_sigmoid(a, w, bias, name, col0=0):
    m, k = a.shape
    n = w.shape[1] - col0
    tm = min(MM_TILE, m)
    tn = min(MM_TILE, n)
    cb0 = col0 // tn
    return pl.pallas_call(
        _mm_sigmoid_kernel,
        grid=(m // tm, n // tn),
        in_specs=[pl.BlockSpec((tm, k), lambda i, j: (i, 0)),
                  pl.BlockSpec((k, tn), lambda i, j: (0, cb0 + j)),
                  pl.BlockSpec((1, tn), lambda i, j: (0, j))],
        out_specs=pl.BlockSpec((tm, tn), lambda i, j: (i, j)),
        out_shape=jax.ShapeDtypeStruct((m, n), BF16),
        compiler_params=_cparams(("parallel", "parallel")),
        name=name,
    )(a, w, bias)


def _mm_res_kernel(a_ref, w_ref, r_ref, o_ref):
    k = pl.program_id(2)

    @pl.when(k == 0)
    def _():
        o_ref[...] = r_ref[...]

    o_ref[...] += _dot(a_ref[...], w_ref[...])


def _matmul_residual(a, w, res, tk, name):
    m, k = a.shape
    n = w.shape[1]
    tm = min(MM_TILE, m)
    tn = min(MM_TILE, n)
    return pl.pallas_call(
        _mm_res_kernel,
        grid=(m // tm, n // tn, k // tk),
        in_specs=[pl.BlockSpec((tm, tk), lambda i, j, kk: (i, kk)),
                  pl.BlockSpec((tk, tn), lambda i, j, kk: (kk, j)),
                  pl.BlockSpec((tm, tn), lambda i, j, kk: (i, j))],
        out_specs=pl.BlockSpec((tm, tn), lambda i, j, kk: (i, j)),
        out_shape=jax.ShapeDtypeStruct((m, n), F32),
        compiler_params=_cparams(("parallel", "parallel", "arbitrary")),
        name=name,
    )(a, w, res)


def _ffn_gu_kernel(a_ref, wg_ref, wu_ref, o_ref):
    a = a_ref[...]
    g = _dot(a, wg_ref[...])
    u = _dot(a, wu_ref[...])
    o_ref[...] = (g * jax.nn.sigmoid(g) * u).astype(o_ref.dtype)


def _ffn_gate_up(a, wg, wu):
    m, k = a.shape
    n = wg.shape[1]
    tm = min(MM_TILE, m)
    tn = MM_TILE // 2
    return pl.pallas_call(
        _ffn_gu_kernel,
        grid=(m // tm, n // tn),
        in_specs=[pl.BlockSpec((tm, k), lambda i, j: (i, 0)),
                  pl.BlockSpec((k, tn), lambda i, j: (0, j)),
                  pl.BlockSpec((k, tn), lambda i, j: (0, j))],
        out_specs=pl.BlockSpec((tm, tn), lambda i, j: (i, j)),
        out_shape=jax.ShapeDtypeStruct((m, n), BF16),
        compiler_params=_cparams(("parallel", "parallel")),
        name="ffn_gate_up",
    )(a, wg, wu)


def _merge_kernel(ya_ref, yb_ref, yc_ref, wa_ref, wb_ref, wc_ref, ga_ref, gb_ref, gc_ref, o_ref):
    acc = ga_ref[...].astype(F32) * _dot(ya_ref[...], wa_ref[...])
    acc += gb_ref[...].astype(F32) * _dot(yb_ref[...], wb_ref[...])
    acc += gc_ref[...].astype(F32) * _dot(yc_ref[...], wc_ref[...])
    o_ref[...] = acc.astype(o_ref.dtype)


def _merge(ya, yb, yc, wa, wb, wc, gate):
    m, c = ya.shape
    d = wa.shape[1]
    tm = min(MM_TILE, m)
    tn = MM_TILE // 2
    nb = d // tn
    yspec = pl.BlockSpec((tm, c), lambda i, j: (i, 0))
    wspec = pl.BlockSpec((c, tn), lambda i, j: (0, j))

    def gspec(br):
        return pl.BlockSpec((tm, tn), lambda i, j: (i, br * nb + j))

    return pl.pallas_call(
        _merge_kernel,
        grid=(m // tm, nb),
        in_specs=[yspec, yspec, yspec, wspec, wspec, wspec, gspec(0), gspec(1), gspec(2)],
        out_specs=pl.BlockSpec((tm, tn), lambda i, j: (i, j)),
        out_shape=jax.ShapeDtypeStruct((m, d), BF16),
        compiler_params=_cparams(("parallel", "parallel")),
        name="branch_merge",
    )(ya, yb, yc, wa, wb, wc, gate, gate, gate)


def _dft_tables(seq_len):
    f = jnp.arange(seq_len, dtype=jnp.int32)

    def trig(s):
        ang = ((f[:, None] * s[None, :]) % (2 * seq_len)).astype(F32) * (math.pi / seq_len)
        return jnp.cos(ang), jnp.sin(ang)

    c0, s0 = trig(jnp.arange(DFT_SPLIT, dtype=jnp.int32))
    c1, s1 = trig(jnp.arange(seq_len // DFT_SPLIT, dtype=jnp.int32) * DFT_SPLIT)
    cm = c1[:, :, None] * c0[:, None, :] - s1[:, :, None] * s0[:, None, :]
    sm = s1[:, :, None] * c0[:, None, :] + c1[:, :, None] * s0[:, None, :]
    return cm.reshape(seq_len, seq_len).astype(BF16), sm.reshape(seq_len, seq_len).astype(BF16)


def _filter_features(seq_len):
    pos = jnp.arange(seq_len, dtype=F32)
    t = pos / (seq_len - 1)
    omega = 2.0 * math.pi * pos / seq_len
    bands = jnp.linspace(1e-4, HY_BANDS - 1, HY_BANDS, dtype=F32)
    ang = omega[:, None] * bands[None, :]
    feat = jnp.concatenate([t[:, None], jnp.cos(ang), -jnp.sin(ang)], axis=-1)
    return t, feat


def _filter_mlp_kernel(feat_ref, w1_ref, b1_ref, w2_ref, b2_ref, freq_ref, o_ref):
    freq = freq_ref[...]
    hdn = jnp.sin(freq * (_dot_hi(feat_ref[...], w1_ref[...]) + b1_ref[...]))
    for j in range(HY_N_INNER):
        hdn = jnp.sin(freq * (_dot_hi(hdn, w2_ref[j]) + b2_ref[j]))
    o_ref[...] = hdn


def _filter_mlp(feat_pad, w1_pad, b1, w2, b2, freq):
    seq_len = feat_pad.shape[0]
    hid = w1_pad.shape[1]
    return pl.pallas_call(
        _filter_mlp_kernel,
        out_shape=jax.ShapeDtypeStruct((seq_len, hid), F32),
        name="hyena_filter_mlp",
    )(feat_pad, w1_pad, b1.reshape(1, hid), w2, b2.reshape(HY_N_INNER, 1, hid), freq.reshape(1, hid))


def _filter_spec_kernel(hdn_ref, wf_ref, wb_ref, t_ref, dl_ref, cm_ref, sm_ref,
                        kre_ref, kim_ref, kl_ref):
    seq_len = hdn_ref.shape[0]
    hdn = hdn_ref[...]
    win = jnp.exp(-t_ref[...] * dl_ref[...])
    row = lax.broadcasted_iota(jnp.int32, (seq_len, 1), 0)
    fwd = _dot_hi(hdn, wf_ref[...]) * win
    bwd = jnp.where(row == 0, 0.0, _dot_hi(hdn, wb_ref[...]) * win)
    nrm = lax.rsqrt(jnp.sum(fwd * fwd, axis=0, keepdims=True)
                    + jnp.sum(bwd * bwd, axis=0, keepdims=True))
    even = (fwd + bwd) * nrm
    odd = (bwd - fwd) * nrm
    sgn = (1 - 2 * (row & 1)).astype(F32)
    wfreq = jnp.where(row == 0, 0.5 / seq_len, 1.0 / seq_len)
    kre_ref[0] = (_dot(cm_ref[...], even.astype(BF16)) * wfreq).astype(kre_ref.dtype)
    kim_ref[0] = (_dot(sm_ref[...], odd.astype(BF16)) * wfreq).astype(kim_ref.dtype)
    kl_ref[0] = jnp.sum(sgn * even, axis=0, keepdims=True) * (0.5 / seq_len)


def _filter_spectra(hdn, w3, t_col, deltas, cm, sm):
    seq_len, hid = hdn.shape
    c = deltas.shape[1]
    tc = HY_CTILE
    nct = c // tc
    const2 = lambda o, j: (0, 0)
    return pl.pallas_call(
        _filter_spec_kernel,
        grid=(HY_ORDER, nct),
        in_specs=[pl.BlockSpec((seq_len, hid), const2),
                  pl.BlockSpec((hid, tc), lambda o, j: (0, o * 2 * nct + j)),
                  pl.BlockSpec((hid, tc), lambda o, j: (0, o * 2 * nct + nct + j)),
                  pl.BlockSpec((seq_len, 1), const2),
                  pl.BlockSpec((1, tc), lambda o, j: (0, j)),
                  pl.BlockSpec((seq_len, seq_len), const2, pipeline_mode=pl.Buffered(1)),
                  pl.BlockSpec((seq_len, seq_len), const2, pipeline_mode=pl.Buffered(1))],
        out_specs=[pl.BlockSpec((1, seq_len, tc), lambda o, j: (o, 0, j)),
                   pl.BlockSpec((1, seq_len, tc), lambda o, j: (o, 0, j)),
                   pl.BlockSpec((1, 1, tc), lambda o, j: (o, 0, j))],
        out_shape=[jax.ShapeDtypeStruct((HY_ORDER, seq_len, c), BF16),
                   jax.ShapeDtypeStruct((HY_ORDER, seq_len, c), BF16),
                   jax.ShapeDtypeStruct((HY_ORDER, 1, c), F32)],
        compiler_params=_cparams(("parallel", "parallel")),
        name="hyena_filter_spectra",
    )(hdn, w3, w3, t_col, deltas, cm, sm)


def _short_conv(x, w, b, seq_len):
    row = lax.broadcasted_iota(jnp.int32, (seq_len, 1), 0)
    prev = jnp.where(row == 0, 0.0, pltpu.roll(x, 1, 0))
    nxt = jnp.where(row == seq_len - 1, 0.0, pltpu.roll(x, seq_len - 1, 0))
    return prev * w[0:1] + x * w[1:2] + nxt * w[2:3] + b


def _hyena_kernel(x1_ref, x2_ref, v_ref, w1_ref, w2_ref, wv_ref, b1_ref, b2_ref, bv_ref,
                  skip_ref, kre_ref, kim_ref, kl_ref, cm_ref, sm_ref, o_ref,
                  sig_ref, gate_ref, u_ref, p_ref):
    seq_len = v_ref.shape[1]
    nchunk = seq_len // HY_ROWS
    row = lax.broadcasted_iota(jnp.int32, (seq_len, 1), 0)
    sgn = (1 - 2 * (row & 1)).astype(F32)

    def long_conv(order, epilogue):
        sig = sig_ref[...]
        sig_b = sig.astype(BF16)
        nyq = jnp.sum(sig * sgn, axis=0, keepdims=True) * kl_ref[order]
        for ch in range(nchunk):
            rows = slice(ch * HY_ROWS, (ch + 1) * HY_ROWS)
            re = _dot(cm_ref[rows, :], sig_b)
            im = _dot(sm_ref[rows, :], sig_b)
            kre = kre_ref[order, rows, :].astype(F32)
            kim = kim_ref[order, rows, :].astype(F32)
            u_ref[rows, :] = (re * kre + im * kim).astype(BF16)
            p_ref[rows, :] = (im * kre - re * kim).astype(BF16)
        for ch in range(nchunk):
            rows = slice(ch * HY_ROWS, (ch + 1) * HY_ROWS)
            y = _dot(cm_ref[rows, :], u_ref[...]) + _dot(sm_ref[rows, :], p_ref[...])
            epilogue(rows, y + sgn[rows] * nyq)

    sig_ref[...] = _short_conv(v_ref[0], wv_ref[...], bv_ref[...], seq_len)
    gate_ref[...] = _short_conv(x1_ref[0], w1_ref[...], b1_ref[...], seq_len)

    def first(rows, y):
        sig = sig_ref[rows, :]
        sig_ref[rows, :] = gate_ref[rows, :] * (y + skip_ref[0:1, :] * sig)

    long_conv(0, first)
    gate_ref[...] = _short_conv(x2_ref[0], w2_ref[...], b2_ref[...], seq_len)

    def second(rows, y):
        o_ref[0, rows, :] = (gate_ref[rows, :] * (y + skip_ref[1:2, :] * sig_ref[rows, :])).astype(o_ref.dtype)

    long_conv(1, second)


def _hyena(hy, conv_w, conv_b, skip, kre, kim, kl, cm, sm):
    bsz, seq_len, _ = hy.shape
    c = skip.shape[1]
    tc = HY_CTILE
    nct = c // tc

    def xspec(part):
        return pl.BlockSpec((1, seq_len, tc), lambda j, b: (b, 0, part * nct + j))

    def wspec(part):
        return pl.BlockSpec((HY_SHORT, tc), lambda j, b: (0, part * nct + j))

    def bspec(part):
        return pl.BlockSpec((1, tc), lambda j, b: (0, part * nct + j))

    const2 = lambda j, b: (0, 0)
    return pl.pallas_call(
        _hyena_kernel,
        grid=(nct, bsz),
        in_specs=[xspec(0), xspec(1), xspec(2), wspec(0), wspec(1), wspec(2),
                  bspec(0), bspec(1), bspec(2),
                  pl.BlockSpec((HY_ORDER, tc), lambda j, b: (0, j)),
                  pl.BlockSpec((HY_ORDER, seq_len, tc), lambda j, b: (0, 0, j)),
                  pl.BlockSpec((HY_ORDER, seq_len, tc), lambda j, b: (0, 0, j)),
                  pl.BlockSpec((HY_ORDER, 1, tc), lambda j, b: (0, 0, j)),
                  pl.BlockSpec((seq_len, seq_len), const2, pipeline_mode=pl.Buffered(1)),
                  pl.BlockSpec((seq_len, seq_len), const2, pipeline_mode=pl.Buffered(1))],
        out_specs=pl.BlockSpec((1, seq_len, tc), lambda j, b: (b, 0, j)),
        out_shape=jax.ShapeDtypeStruct((bsz, seq_len, c), BF16),
        scratch_shapes=[pltpu.VMEM((seq_len, tc), F32), pltpu.VMEM((seq_len, tc), F32),
                        pltpu.VMEM((seq_len, tc), BF16), pltpu.VMEM((seq_len, tc), BF16)],
        compiler_params=_cparams(("parallel", "arbitrary")),
        name="hyena_conv",
    )(hy, hy, hy, conv_w, conv_w, conv_w, conv_b, conv_b, conv_b, skip, kre, kim, kl, cm, sm)


def _mlstm_kernel(q_ref, k_ref, v_ref, og_ref, kt_ref, gc_ref, gr_ref, bc_ref, br_ref, nw_ref,
                  o_ref, hf_ref, hb_ref, *, chunk, nchunks, dk, dv):
    rowi = lax.broadcasted_iota(jnp.int32, (chunk, chunk), 0)
    coli = lax.broadcasted_iota(jnp.int32, (chunk, chunk), 1)
    lower = coli <= rowi
    upper = coli >= rowi
    bias_c = bc_ref[0]
    bias_r = br_ref[0]
    scale = dk ** -0.5
    neg_inf = -jnp.inf

    def step(c, direction, state):
        c_st, n_st, m_st = state
        r0 = pl.multiple_of(c * chunk, chunk)
        gcol = gc_ref[0, 0, pl.ds(r0, chunk), :] + bias_c
        grow = gr_ref[0, 0, c] + bias_r
        gi, gf = 2 * direction, 2 * direction + 1
        i_col = gcol[:, gi:gi + 1]
        f_col = jax.nn.log_sigmoid(gcol[:, gf:gf + 1])
        i_row = grow[gi:gi + 1, :]
        f_row = jax.nn.log_sigmoid(grow[gf:gf + 1, :])
        mask, mask_t = (lower, upper) if direction == 0 else (upper, lower)
        b_col = jnp.sum(jnp.where(mask, f_row, 0.0), axis=1, keepdims=True)
        b_row = jnp.sum(jnp.where(mask_t, f_col, 0.0), axis=0, keepdims=True)
        g = jnp.sum(f_row, axis=1, keepdims=True)
        dmat = jnp.where(mask, b_col - b_row + i_row, neg_inf)
        inter = b_col + m_st
        m_j = jnp.maximum(inter, jnp.max(dmat, axis=1, keepdims=True))
        w_intra = jnp.exp(dmat - m_j)
        w_inter = jnp.exp(inter - m_j)
        qc = q_ref[0, pl.ds(r0, chunk), :] * scale
        qb = qc.astype(BF16)
        kt = kt_ref[0, 0, c].astype(BF16)
        kc = k_ref[0, pl.ds(r0, chunk), :]
        vc = v_ref[0, pl.ds(r0, chunk), :]
        s = _dot(qb, kt) * w_intra
        num = w_inter * _dot(qb, c_st.astype(BF16)) + _dot(s.astype(BF16), vc.astype(BF16))
        den = (w_inter * jnp.sum(qc * n_st, axis=1, keepdims=True)
               + jnp.sum(s, axis=1, keepdims=True))
        h = num / jnp.maximum(jnp.abs(den), jnp.exp(-m_j))
        lw = g - b_col + i_col
        m_new = jnp.maximum(g + m_st, jnp.max(lw, axis=0, keepdims=True))
        wl = jnp.exp(lw - m_new)
        decay = jnp.exp(g + m_st - m_new)
        c_new = decay * c_st + _dot(kt, (wl * vc).astype(BF16))
        n_new = decay * n_st + jnp.sum(wl * kc, axis=0, keepdims=True)
        return h, (c_new, n_new, m_new)

    def body(c, carry):
        st_f, st_b = carry
        h_f, st_f = step(c, 0, st_f)
        hf_ref[pl.ds(pl.multiple_of(c * chunk, chunk), chunk), :] = h_f
        cb = nchunks - 1 - c
        h_b, st_b = step(cb, 1, st_b)
        hb_ref[pl.ds(pl.multiple_of(cb * chunk, chunk), chunk), :] = h_b
        return st_f, st_b

    zero = (jnp.zeros((dk, dv), F32), jnp.zeros((1, dk), F32), jnp.zeros((1, 1), F32))
    lax.fori_loop(0, nchunks, body, (zero, zero))

    hsum = hf_ref[...] + hb_ref[...]
    hn = hsum * lax.rsqrt(jnp.mean(hsum * hsum, axis=-1, keepdims=True) + EPS) * nw_ref[...]
    o_ref[0] = (jax.nn.sigmoid(og_ref[0]) * hn).astype(o_ref.dtype)


def _mlstm(ml, col0, gates, gate_b, norm_w):
    bsz, seq_len, _ = ml.shape
    nh = ML_HEADS
    dv = norm_w.shape[0] // nh
    dk = dv // 2
    chunk = min(ML_CHUNK, seq_len)
    nchunks = seq_len // chunk
    koff = col0 + nh * dk
    qb0, vb0 = col0 // dk, (col0 + 2 * nh * dk) // dv
    k_t = (ml[:, :, koff:koff + nh * dk].reshape(bsz, nchunks, chunk, nh, dk)
           .transpose(0, 3, 1, 4, 2))
    g4 = gates.reshape(bsz, seq_len, 4, nh)
    g_col = g4.transpose(0, 3, 1, 2)
    g_row = g4.reshape(bsz, nchunks, chunk, 4, nh).transpose(0, 4, 1, 3, 2)
    gb = gate_b.astype(F32).reshape(4, nh)
    b_col = gb.T.reshape(nh, 1, 4)
    b_row = gb.T.reshape(nh, 4, 1)
    kern = functools.partial(_mlstm_kernel, chunk=chunk, nchunks=nchunks, dk=dk, dv=dv)
    return pl.pallas_call(
        kern,
        grid=(bsz, nh),
        in_specs=[pl.BlockSpec((1, seq_len, dk), lambda b, h: (b, 0, qb0 + h)),
                  pl.BlockSpec((1, seq_len, dk), lambda b, h: (b, 0, qb0 + nh + h)),
                  pl.BlockSpec((1, seq_len, dv), lambda b, h: (b, 0, vb0 + h)),
                  pl.BlockSpec((1, seq_len, dv), lambda b, h: (b, 0, vb0 + nh + h)),
                  pl.BlockSpec((1, 1, nchunks, dk, chunk), lambda b, h: (b, h, 0, 0, 0)),
                  pl.BlockSpec((1, 1, seq_len, 4), lambda b, h: (b, h, 0, 0)),
                  pl.BlockSpec((1, 1, nchunks, 4, chunk), lambda b, h: (b, h, 0, 0, 0)),
                  pl.BlockSpec((1, 1, 4), lambda b, h: (h, 0, 0)),
                  pl.BlockSpec((1, 4, 1), lambda b, h: (h, 0, 0)),
                  pl.BlockSpec((1, dv), lambda b, h: (0, h))],
        out_specs=pl.BlockSpec((1, seq_len, dv), lambda b, h: (b, 0, h)),
        out_shape=jax.ShapeDtypeStruct((bsz, seq_len, nh * dv), BF16),
        scratch_shapes=[pltpu.VMEM((seq_len, dv), F32), pltpu.VMEM((seq_len, dv), F32)],
        compiler_params=_cparams(("parallel", "parallel")),
        name="mlstm",
    )(ml, ml, ml, ml, k_t, g_col, g_row, b_col, b_row, norm_w.reshape(1, nh * dv).astype(F32))


def _rglru_kernel(x_ref, y_ref, cw_ref, cb_ref, wa_ref, wx_ref, ba_ref, bx_ref, lam_ref,
                  o_ref, a_ref, b_ref, h_ref):
    seq_len, width = x_ref.shape[1], x_ref.shape[2]
    nblk = seq_len // SUBLANE
    x = x_ref[0]
    row = lax.broadcasted_iota(jnp.int32, (seq_len, 1), 0)
    cw = cw_ref[...]
    xc = (jnp.where(row < 2, 0.0, pltpu.roll(x, 2, 0)) * cw[0:1]
          + jnp.where(row < 1, 0.0, pltpu.roll(x, 1, 0)) * cw[1:2]
          + x * cw[2:3]
          + jnp.where(row == seq_len - 1, 0.0, pltpu.roll(x, seq_len - 1, 0)) * cw[3:4]
          + cb_ref[...])
    xb = xc.astype(BF16)
    row8 = lax.broadcasted_iota(jnp.int32, (SUBLANE, width), 0)

    for direction in range(2):
        r = jax.nn.sigmoid(_dot(xb, wa_ref[direction, 0]) + ba_ref[direction:direction + 1, :])
        ig = jax.nn.sigmoid(_dot(xb, wx_ref[direction, 0]) + bx_ref[direction:direction + 1, :])
        log_a = -RG_C * r * jax.nn.softplus(-lam_ref[direction:direction + 1, :])
        a_ref[...] = jnp.exp(log_a)
        th = jnp.tanh(log_a)
        b_ref[...] = jnp.sqrt(-2.0 * th / (1.0 - th)) * (ig * xc)

        def block(i, carry, direction=direction):
            blk = i if direction == 0 else nblk - 1 - i
            r0 = pl.multiple_of(blk * SUBLANE, SUBLANE)
            a = a_ref[pl.ds(r0, SUBLANE), :]
            b = b_ref[pl.ds(r0, SUBLANE), :]
            for d in (1, 2, 4):
                if direction == 0:
                    shift, valid = d, row8 >= d
                else:
                    shift, valid = SUBLANE - d, row8 < SUBLANE - d
                a_s = jnp.where(valid, pltpu.roll(a, shift, 0), 1.0)
                b_s = jnp.where(valid, pltpu.roll(b, shift, 0), 0.0)
                b = a * b_s + b
                a = a * a_s
            h = b + a * carry
            if direction == 0:
                h_ref[pl.ds(r0, SUBLANE), :] = h
                return h[SUBLANE - 1:SUBLANE, :]
            h_ref[pl.ds(r0, SUBLANE), :] += h
            return h[0:1, :]

        lax.fori_loop(0, nblk, block, jnp.zeros((1, width), F32), unroll=4)

    o_ref[0] = (h_ref[...] * jax.nn.gelu(y_ref[0])).astype(o_ref.dtype)


def _rglru(rg, conv_w, conv_b, wa, ba, wx, bx, lam):
    bsz, seq_len, w2 = rg.shape
    width = w2 // 2
    nh = RG_HEADS
    blk = width // nh
    wspec = pl.BlockSpec((2, 1, blk, blk), lambda b, h: (0, h, 0, 0))
    vspec = pl.BlockSpec((2, blk), lambda b, h: (0, h))
    return pl.pallas_call(
        _rglru_kernel,
        grid=(bsz, nh),
        in_specs=[pl.BlockSpec((1, seq_len, blk), lambda b, h: (b, 0, h)),
                  pl.BlockSpec((1, seq_len, blk), lambda b, h: (b, 0, nh + h)),
                  pl.BlockSpec((RG_CONV, blk), lambda b, h: (0, h)),
                  pl.BlockSpec((1, blk), lambda b, h: (0, h)),
                  wspec, wspec, vspec, vspec, vspec],
        out_specs=pl.BlockSpec((1, seq_len, blk), lambda b, h: (b, 0, h)),
        out_shape=jax.ShapeDtypeStruct((bsz, seq_len, width), BF16),
        scratch_shapes=[pltpu.VMEM((seq_len, blk), F32)] * 3,
        compiler_params=_cparams(("parallel", "parallel")),
        name="rglru",
    )(rg, rg, conv_w.astype(F32), conv_b.reshape(1, width).astype(F32),
      wa.astype(BF16), wx.astype(BF16), ba.astype(F32), bx.astype(F32), lam.astype(F32))


def _pad_cols(w, n):
    return jnp.pad(w, ((0, 0), (0, n - w.shape[1])))


def kernel(x, mix_norm, w_in, b_gate, hy_conv_w, hy_conv_b, hy_w1, hy_b1, hy_w2, hy_b2, hy_freq, hy_w3, hy_skip, ml_gate_b, ml_norm, rg_conv_w, rg_conv_b, rg_wa, rg_ba, rg_wx, rg_bx, rg_lambda, w_br_a, w_br_b, w_br_c, w_out, ffn_norm, w_gate, w_up, w_down, final_norm):
    bsz, seq_len, d_model = x.shape
    depth = mix_norm.shape[0]
    m = bsz * seq_len
    d_mix = d_model // 2
    dv = d_mix // ML_HEADS
    dk = dv // 2
    n_hy = (HY_ORDER + 1) * d_mix
    n_ml = 2 * ML_HEADS * dk + 2 * ML_HEADS * dv
    n_gt = 4 * ML_HEADS
    n_rg = 2 * d_mix
    o_ml = n_hy
    o_gt = o_ml + n_ml
    o_rg = o_gt + n_gt
    o_g = o_rg + n_rg
    ffn_hidden = w_gate.shape[2]
    ffn_pad = -(-ffn_hidden // FFN_PAD) * FFN_PAD

    cm, sm = _dft_tables(seq_len)
    t_pos, feat = _filter_features(seq_len)
    feat_pad = _pad_cols(feat, LANE)
    deltas = jnp.abs(jnp.linspace(math.log(HY_DECAY_TARGET) / HY_SLOW_DECAY,
                                  math.log(HY_DECAY_TARGET) / HY_FAST_DECAY,
                                  d_mix, dtype=F32)).reshape(1, d_mix)

    w_in_t = jnp.swapaxes(w_in, 1, 2)
    h = x.reshape(m, d_model).astype(F32)
    for l in range(depth):
        u = _rmsnorm(h, mix_norm[l], BF16)
        hm = _matmul(u, _cast_weight_t(w_in_t, l, 0, o_gt), F32, "in_proj_hyena_mlstm")
        hm = hm.reshape(bsz, seq_len, o_gt)
        gt = _matmul(u, _cast_weight_t(w_in_t, l, o_gt, n_gt), F32, "in_proj_mlstm_gates")
        w_tail = _cast_weight_t(w_in_t, l, o_rg, n_rg + N_BRANCH * d_model)
        rg = _matmul(u, w_tail, F32, "in_proj_rglru", 0, n_rg)
        gate = _matmul_sigmoid(u, w_tail, b_gate[l].reshape(1, N_BRANCH * d_model).astype(F32),
                               "in_proj_merge_gates", n_rg)

        w1_pad = jnp.pad(hy_w1[l].astype(F32), ((0, LANE - HY_EMB), (0, 0)))
        hdn = _filter_mlp(feat_pad, w1_pad, hy_b1[l].astype(F32), hy_w2[l].astype(F32),
                          hy_b2[l].astype(F32), hy_freq[l].astype(F32))
        kre, kim, kl = _filter_spectra(hdn, hy_w3[l].astype(F32), t_pos.reshape(seq_len, 1), deltas, cm, sm)
        y_a = _hyena(hm, hy_conv_w[l].astype(F32),
                     hy_conv_b[l].reshape(1, n_hy).astype(F32), hy_skip[l].astype(F32),
                     kre, kim, kl, cm, sm)

        y_b = _mlstm(hm, o_ml, gt.reshape(bsz, seq_len, n_gt), ml_gate_b[l], ml_norm[l])
        y_c = _rglru(rg.reshape(bsz, seq_len, n_rg), rg_conv_w[l], rg_conv_b[l],
                     rg_wa[l], rg_ba[l], rg_wx[l], rg_bx[l], rg_lambda[l])

        merged = _merge(y_a.reshape(m, d_mix), y_b.reshape(m, d_mix), y_c.reshape(m, d_mix),
                        _cast_weight(w_br_a, l), _cast_weight(w_br_b, l), _cast_weight(w_br_c, l), gate)
        h = _matmul_residual(merged, _cast_weight(w_out, l), h, d_model // 2, "out_proj")

        u = _rmsnorm(h, ffn_norm[l], BF16)
        ff = _ffn_gate_up(u, _cast_weight(w_gate, l, out_cols=ffn_pad),
                          _cast_weight(w_up, l, out_cols=ffn_pad))
        h = _matmul_residual(ff, _cast_weight(w_down, l, out_rows=ffn_pad), h, ffn_pad // 4, "ffn_down")

    out = _rmsnorm(h, final_norm, x.dtype)
    return out.reshape(bsz, seq_len, d_model)
```

```python
import functools
import math
from typing import NamedTuple

import jax
import jax.numpy as jnp
from jax import lax
from jax.experimental import pallas as pl
from jax.experimental.pallas import tpu as pltpu

F32 = jnp.float32
BF16 = jnp.bfloat16

N_BRANCH = 3
HY_ORDER = 2
HY_SHORT = 3
HY_EMB = 33
HY_BANDS = (HY_EMB - 1) // 2
HY_FILTER_HIDDEN = 64
HY_N_INNER = 2
HY_FAST_DECAY = 0.3
HY_SLOW_DECAY = 1.5
HY_DECAY_TARGET = 1e-2
ML_HEADS = 8
RG_HEADS = 8
RG_CONV = 4
RG_C = 8.0
EPS = 1e-6

V7X_VMEM_LIMIT_BYTES = 56 * 1024 * 1024
LANE = 128
SUBLANE = 8
MM_TILE = 1024
ML_CHUNK = 256
HY_CTILE = 256
HY_ROWS = 512
FFN_PAD = 1024
CAST_BLOCK_ELEMS = 1024 * 1024
DFT_SPLIT = 64


def _cparams(sem, vmem=V7X_VMEM_LIMIT_BYTES):
    return pltpu.CompilerParams(dimension_semantics=sem, vmem_limit_bytes=vmem)


def _dot(a, b):
    return jnp.dot(a, b, preferred_element_type=F32)


def _dot_hi(a, b):
    return jnp.dot(a, b, preferred_element_type=F32, precision=lax.Precision.HIGHEST)


def _cast_kernel(x_ref, o_ref, *, src_row_blocks, src_col_blocks):
    inside = (pl.program_id(0) < src_row_blocks) & (pl.program_id(1) < src_col_blocks)
    o_ref[...] = jnp.where(inside, x_ref[...], 0.0).astype(o_ref.dtype)


def _cast_weight(w_stack, layer, col0=0, ncols=None, out_rows=None, out_cols=None):
    _, k, n = w_stack.shape
    ncols = n - col0 if ncols is None else ncols
    out_rows = k if out_rows is None else out_rows
    out_cols = ncols if out_cols is None else out_cols
    tc = next(t for t in (4096, 2048, 1024, 512, 256, LANE)
              if ncols % t == 0 and out_cols % t == 0 and col0 % t == 0)
    tr = next(t for t in (2048, 1024, 512, 256)
              if k % t == 0 and out_rows % t == 0 and t * tc <= CAST_BLOCK_ELEMS)
    src_rb, src_cb, cb0 = k // tr, ncols // tc, col0 // tc
    kern = functools.partial(_cast_kernel, src_row_blocks=src_rb, src_col_blocks=src_cb)
    return pl.pallas_call(
        kern,
        grid=(out_rows // tr, out_cols // tc),
        in_specs=[pl.BlockSpec((None, tr, tc), lambda i, j: (layer, jnp.minimum(i, src_rb - 1),
                                                               cb0 + jnp.minimum(j, src_cb - 1)))],
        out_specs=pl.BlockSpec((tr, tc), lambda i, j: (i, j)),
        out_shape=jax.ShapeDtypeStruct((out_rows, out_cols), BF16),
        compiler_params=_cparams(("parallel", "parallel")),
        name="weight_cast",
    )(w_stack)


class _CastJob(NamedTuple):
    src: jax.Array
    layer: int
    axis: int
    block: tuple
    n_src: int
    n_out: int
    group: int

    @property
    def steps(self):
        return self.n_out // self.group

    @property
    def out_block(self):
        return tuple(d * self.group if a == self.axis else d for a, d in enumerate(self.block))

    @property
    def out_shape(self):
        return tuple(d * self.n_out if a == self.axis else d for a, d in enumerate(self.block))


def _cast_job(src, layer, axis, width, padded, host_steps):
    size = src.shape[1 + axis]
    assert size % width == 0 and padded % width == 0
    n_out = padded // width
    group = next(g for g in range(1, n_out + 1) if n_out % g == 0 and n_out // g <= host_steps)
    block = (width, src.shape[2]) if axis == 0 else (src.shape[1], width)
    return _CastJob(src, layer, axis, block, size // width, n_out, group)


def _linear_step(ids, grid):
    t = ids[0]
    for idx, n in zip(ids[1:], grid[1:]):
        t = t * n + idx
    return t


def _call_with_cast(body, job, args, *, grid, in_specs, out_specs, out_shape, scratch_shapes, sem, name):
    n_in = len(in_specs)
    assert job.steps <= math.prod(grid) and job.n_out % job.group == 0

    def local(ids):
        return jnp.minimum(_linear_step(ids, grid), job.steps - 1)

    def src_map(q):
        def index(*ids):
            blk = jnp.minimum(local(ids) * job.group + q, job.n_src - 1)
            return (job.layer, blk, 0) if job.axis == 0 else (job.layer, 0, blk)
        return index

    def dst_map(*ids):
        return (local(ids), 0) if job.axis == 0 else (0, local(ids))

    def kern(*refs):
        srcs = refs[n_in:n_in + job.group]
        host_out, dst = refs[n_in + job.group], refs[n_in + job.group + 1]
        body(*refs[:n_in], host_out, *refs[n_in + job.group + 2:])
        t = _linear_step([pl.program_id(a) for a in range(len(grid))], grid)

        @pl.when(t < job.steps)
        def _():
            parts = [jnp.where(t * job.group + q < job.n_src, src[...], 0.0) for q, src in enumerate(srcs)]
            dst[...] = jnp.concatenate(parts, axis=job.axis).astype(dst.dtype)

    src_specs = [pl.BlockSpec((None,) + tuple(job.block), src_map(q)) for q in range(job.group)]
    return pl.pallas_call(
        kern,
        grid=grid,
        in_specs=list(in_specs) + src_specs,
        out_specs=[out_specs, pl.BlockSpec(job.out_block, dst_map)],
        out_shape=[out_shape, jax.ShapeDtypeStruct(job.out_shape, BF16)],
        scratch_shapes=scratch_shapes,
        compiler_params=_cparams(sem),
        name=name,
    )(*args, *([job.src] * job.group))


def _cast_t_kernel(a_ref, o_ref):
    o_ref[...] = a_ref[...].T.astype(o_ref.dtype)


def _cast_t_shift_kernel(a_ref, b_ref, o_ref, *, shift):
    rows = jnp.concatenate([a_ref[shift:, :], b_ref[...]], axis=0)
    o_ref[...] = rows.T.astype(o_ref.dtype)


def _cast_weight_t(w_t, layer, row0, nrows):
    _, _, k = w_t.shape
    tr = min(CAST_BLOCK_ELEMS // k, nrows)
    shift = row0 % tr
    base = row0 - shift
    out_spec = pl.BlockSpec((k, tr), lambda i: (0, i))
    a_spec = pl.BlockSpec((None, tr, k), lambda i: (layer, base // tr + i, 0))
    common = dict(grid=(nrows // tr,), out_specs=out_spec,
                  out_shape=jax.ShapeDtypeStruct((k, nrows), BF16),
                  compiler_params=_cparams(("parallel",)))
    if shift == 0:
        return pl.pallas_call(_cast_t_kernel, in_specs=[a_spec], name="weight_cast_t", **common)(w_t)
    b_spec = pl.BlockSpec((None, shift, k), lambda i: (layer, (base + (i + 1) * tr) // shift, 0))
    kern = functools.partial(_cast_t_shift_kernel, shift=shift)
    return pl.pallas_call(kern, in_specs=[a_spec, b_spec], name="weight_cast_t_shifted", **common)(w_t, w_t)


def _rms_kernel(x_ref, g_ref, o_ref):
    x = x_ref[...]
    ms = jnp.mean(x * x, axis=-1, keepdims=True)
    o_ref[...] = (x * lax.rsqrt(ms + EPS) * g_ref[...]).astype(o_ref.dtype)


def _rmsnorm(x2d, g, out_dtype):
    m, d = x2d.shape
    tm = min(256, m)
    return pl.pallas_call(
        _rms_kernel,
        grid=(m // tm,),
        in_specs=[pl.BlockSpec((tm, d), lambda i: (i, 0)),
                  pl.BlockSpec((1, d), lambda i: (0, 0))],
        out_specs=pl.BlockSpec((tm, d), lambda i: (i, 0)),
        out_shape=jax.ShapeDtypeStruct((m, d), out_dtype),
        compiler_params=_cparams(("parallel",)),
        name="rmsnorm",
    )(x2d, g.reshape(1, d).astype(F32))


def _mm_kernel(a_ref, w_ref, o_ref):
    o_ref[...] = _dot(a_ref[...], w_ref[...]).astype(o_ref.dtype)


def _matmul(a, w, out_dtype, name, col0=0, ncols=None):
    m, k = a.shape
    n = w.shape[1] - col0 if ncols is None else ncols
    tm = min(MM_TILE, m)
    tn = min(MM_TILE, n)
    cb0 = col0 // tn
    return pl.pallas_call(
        _mm_kernel,
        grid=(m // tm, n // tn),
        in_specs=[pl.BlockSpec((tm, k), lambda i, j: (i, 0)),
                  pl.BlockSpec((k, tn), lambda i, j: (0, cb0 + j))],
        out_specs=pl.BlockSpec((tm, tn), lambda i, j: (i, j)),
        out_shape=jax.ShapeDtypeStruct((m, n), out_dtype),
        compiler_params=_cparams(("parallel", "parallel")),
        name=name,
    )(a, w)


def _mm_sigmoid_kernel(a_ref, w_ref, b_ref, o_ref):
    o_ref[...] = jax.nn.sigmoid(_dot(a_ref[...], w_ref[...]) + b_ref[...]).astype(o_ref.dtype)


def _matmul_sigmoid(a, w, bias, name, col0=0):
    m, k = a.shape
    n = w.shape[1] - col0
    tm = min(MM_TILE, m)
    tn = min(MM_TILE, n)
    cb0 = col0 // tn
    return pl.pallas_call(
        _mm_sigmoid_kernel,
        grid=(m // tm, n // tn),
        in_specs=[pl.BlockSpec((tm, k), lambda i, j: (i, 0)),
                  pl.BlockSpec((k, tn), lambda i, j: (0, cb0 + j)),
                  pl.BlockSpec((1, tn), lambda i, j: (0, j))],
        out_specs=pl.BlockSpec((tm, tn), lambda i, j: (i, j)),
        out_shape=jax.ShapeDtypeStruct((m, n), BF16),
        compiler_params=_cparams(("parallel", "parallel")),
        name=name,
    )(a, w, bias)


def _mm_res_kernel(a_ref, w_ref, r_ref, o_ref):
    k = pl.program_id(2)

    @pl.when(k == 0)
    def _():
        o_ref[...] = r_ref[...]

    o_ref[...] += _dot(a_ref[...], w_ref[...])


def _matmul_residual(a, w, res, tk, name):
    m, k = a.shape
    n = w.shape[1]
    tm = min(MM_TILE, m)
    tn = min(MM_TILE, n)
    return pl.pallas_call(
        _mm_res_kernel,
        grid=(m // tm, n // tn, k // tk),
        in_specs=[pl.BlockSpec((tm, tk), lambda i, j, kk: (i, kk)),
                  pl.BlockSpec((tk, tn), lambda i, j, kk: (kk, j)),
                  pl.BlockSpec((tm, tn), lambda i, j, kk: (i, j))],
        out_specs=pl.BlockSpec((tm, tn), lambda i, j, kk: (i, j)),
        out_shape=jax.ShapeDtypeStruct((m, n), F32),
        compiler_params=_cparams(("parallel", "parallel", "arbitrary")),
        name=name,
    )(a, w, res)


def _mm_res_fullk_kernel(a_ref, w_ref, r_ref, o_ref):
    o_ref[...] = r_ref[...] + _dot(a_ref[...], w_ref[...])


def _matmul_residual_fullk(a, w, res, name):
    m, k = a.shape
    n = w.shape[1]
    tm = min(MM_TILE, m)
    tn = min(MM_TILE // 2, n)
    return pl.pallas_call(
        _mm_res_fullk_kernel,
        grid=(m // tm, n // tn),
        in_specs=[pl.BlockSpec((tm, k), lambda i, j: (i, 0)),
                  pl.BlockSpec((k, tn), lambda i, j: (0, j)),
                  pl.BlockSpec((tm, tn), lambda i, j: (i, j))],
        out_specs=pl.BlockSpec((tm, tn), lambda i, j: (i, j)),
        out_shape=jax.ShapeDtypeStruct((m, n), F32),
        compiler_params=_cparams(("parallel", "parallel")),
        name=name,
    )(a, w, res)


def _ffn_gu_kernel(a_ref, wg_ref, wu_ref, o_ref):
    a = a_ref[...]
    g = _dot(a, wg_ref[...])
    u = _dot(a, wu_ref[...])
    o_ref[...] = (g * jax.nn.sigmoid(g) * u).astype(o_ref.dtype)


def _ffn_gate_up(a, wg, wu, job):
    m, k = a.shape
    n = wg.shape[1]
    tm = min(MM_TILE, m)
    tn = MM_TILE // 2
    return _call_with_cast(
        _ffn_gu_kernel, job, (a, wg, wu),
        grid=(m // tm, n // tn),
        in_specs=[pl.BlockSpec((tm, k), lambda i, j: (i, 0)),
                  pl.BlockSpec((k, tn), lambda i, j: (0, j)),
                  pl.BlockSpec((k, tn), lambda i, j: (0, j))],
        out_specs=pl.BlockSpec((tm, tn), lambda i, j: (i, j)),
        out_shape=jax.ShapeDtypeStruct((m, n), BF16),
        scratch_shapes=[],
        sem=("arbitrary", "arbitrary"),
        name="ffn_gate_up",
    )


def _merge_kernel(ya_ref, yb_ref, yc_ref, wa_ref, wb_ref, wc_ref, ga_ref, gb_ref, gc_ref, o_ref):
    acc = ga_ref[...].astype(F32) * _dot(ya_ref[...], wa_ref[...])
    acc += gb_ref[...].astype(F32) * _dot(yb_ref[...], wb_ref[...])
    acc += gc_ref[...].astype(F32) * _dot(yc_ref[...], wc_ref[...])
    o_ref[...] = acc.astype(o_ref.dtype)


def _merge(ya, yb, yc, wa, wb, wc, gate):
    m, c = ya.shape
    d = wa.shape[1]
    tm = min(MM_TILE, m)
    tn = MM_TILE // 2
    nb = d // tn
    yspec = pl.BlockSpec((tm, c), lambda i, j: (i, 0))
    wspec = pl.BlockSpec((c, tn), lambda i, j: (0, j))

    def gspec(br):
        return pl.BlockSpec((tm, tn), lambda i, j: (i, br * nb + j))

    return pl.pallas_call(
        _merge_kernel,
        grid=(m // tm, nb),
        in_specs=[yspec, yspec, yspec, wspec, wspec, wspec, gspec(0), gspec(1), gspec(2)],
        out_specs=pl.BlockSpec((tm, tn), lambda i, j: (i, j)),
        out_shape=jax.ShapeDtypeStruct((m, d), BF16),
        compiler_params=_cparams(("parallel", "parallel")),
        name="branch_merge",
    )(ya, yb, yc, wa, wb, wc, gate, gate, gate)


def _dft_tables(seq_len):
    f = jnp.arange(seq_len, dtype=jnp.int32)

    def trig(s):
        ang = ((f[:, None] * s[None, :]) % (2 * seq_len)).astype(F32) * (math.pi / seq_len)
        return jnp.cos(ang), jnp.sin(ang)

    c0, s0 = trig(jnp.arange(DFT_SPLIT, dtype=jnp.int32))
    c1, s1 = trig(jnp.arange(seq_len // DFT_SPLIT, dtype=jnp.int32) * DFT_SPLIT)
    cm = c1[:, :, None] * c0[:, None, :] - s1[:, :, None] * s0[:, None, :]
    sm = s1[:, :, None] * c0[:, None, :] + c1[:, :, None] * s0[:, None, :]
    return cm.reshape(seq_len, seq_len).astype(BF16), sm.reshape(seq_len, seq_len).astype(BF16)


def _filter_features(seq_len):
    pos = jnp.arange(seq_len, dtype=F32)
    t = pos / (seq_len - 1)
    omega = 2.0 * math.pi * pos / seq_len
    bands = jnp.linspace(1e-4, HY_BANDS - 1, HY_BANDS, dtype=F32)
    ang = omega[:, None] * bands[None, :]
    feat = jnp.concatenate([t[:, None], jnp.cos(ang), -jnp.sin(ang)], axis=-1)
    return t, feat


def _filter_mlp_kernel(feat_ref, w1_ref, b1_ref, w2_ref, b2_ref, freq_ref, o_ref):
    freq = freq_ref[...]
    hdn = jnp.sin(freq * (_dot_hi(feat_ref[...], w1_ref[...]) + b1_ref[...]))
    for j in range(HY_N_INNER):
        hdn = jnp.sin(freq * (_dot_hi(hdn, w2_ref[j]) + b2_ref[j]))
    o_ref[...] = hdn


def _filter_mlp(feat_pad, w1_pad, b1, w2, b2, freq):
    seq_len = feat_pad.shape[0]
    hid = w1_pad.shape[1]
    return pl.pallas_call(
        _filter_mlp_kernel,
        out_shape=jax.ShapeDtypeStruct((seq_len, hid), F32),
        name="hyena_filter_mlp",
    )(feat_pad, w1_pad, b1.reshape(1, hid), w2, b2.reshape(HY_N_INNER, 1, hid), freq.reshape(1, hid))


def _filter_spec_kernel(hdn_ref, wf_ref, wb_ref, t_ref, dl_ref, cm_ref, sm_ref,
                        kre_ref, kim_ref, kl_ref):
    seq_len = hdn_ref.shape[0]
    hdn = hdn_ref[...]
    win = jnp.exp(-t_ref[...] * dl_ref[...])
    row = lax.broadcasted_iota(jnp.int32, (seq_len, 1), 0)
    fwd = _dot_hi(hdn, wf_ref[...]) * win
    bwd = jnp.where(row == 0, 0.0, _dot_hi(hdn, wb_ref[...]) * win)
    nrm = lax.rsqrt(jnp.sum(fwd * fwd, axis=0, keepdims=True)
                    + jnp.sum(bwd * bwd, axis=0, keepdims=True))
    even = (fwd + bwd) * nrm
    odd = (bwd - fwd) * nrm
    sgn = (1 - 2 * (row & 1)).astype(F32)
    wfreq = jnp.where(row == 0, 0.5 / seq_len, 1.0 / seq_len)
    kre_ref[0] = (_dot(cm_ref[...], even.astype(BF16)) * wfreq).astype(kre_ref.dtype)
    kim_ref[0] = (_dot(sm_ref[...], odd.astype(BF16)) * wfreq).astype(kim_ref.dtype)
    kl_ref[0] = jnp.sum(sgn * even, axis=0, keepdims=True) * (0.5 / seq_len)


def _filter_spectra(hdn, w3, t_col, deltas, cm, sm):
    seq_len, hid = hdn.shape
    c = deltas.shape[1]
    tc = HY_CTILE
    nct = c // tc
    const2 = lambda o, j: (0, 0)
    return pl.pallas_call(
        _filter_spec_kernel,
        grid=(HY_ORDER, nct),
        in_specs=[pl.BlockSpec((seq_len, hid), const2),
                  pl.BlockSpec((hid, tc), lambda o, j: (0, o * 2 * nct + j)),
                  pl.BlockSpec((hid, tc), lambda o, j: (0, o * 2 * nct + nct + j)),
                  pl.BlockSpec((seq_len, 1), const2),
                  pl.BlockSpec((1, tc), lambda o, j: (0, j)),
                  pl.BlockSpec((seq_len, seq_len), const2, pipeline_mode=pl.Buffered(1)),
                  pl.BlockSpec((seq_len, seq_len), const2, pipeline_mode=pl.Buffered(1))],
        out_specs=[pl.BlockSpec((1, seq_len, tc), lambda o, j: (o, 0, j)),
                   pl.BlockSpec((1, seq_len, tc), lambda o, j: (o, 0, j)),
                   pl.BlockSpec((1, 1, tc), lambda o, j: (o, 0, j))],
        out_shape=[jax.ShapeDtypeStruct((HY_ORDER, seq_len, c), BF16),
                   jax.ShapeDtypeStruct((HY_ORDER, seq_len, c), BF16),
                   jax.ShapeDtypeStruct((HY_ORDER, 1, c), F32)],
        compiler_params=_cparams(("parallel", "parallel")),
        name="hyena_filter_spectra",
    )(hdn, w3, w3, t_col, deltas, cm, sm)


def _short_conv(x, w, b, seq_len):
    row = lax.broadcasted_iota(jnp.int32, (seq_len, 1), 0)
    prev = jnp.where(row == 0, 0.0, pltpu.roll(x, 1, 0))
    nxt = jnp.where(row == seq_len - 1, 0.0, pltpu.roll(x, seq_len - 1, 0))
    return prev * w[0:1] + x * w[1:2] + nxt * w[2:3] + b


def _hyena_kernel(x1_ref, x2_ref, v_ref, w1_ref, w2_ref, wv_ref, b1_ref, b2_ref, bv_ref,
                  skip_ref, kre_ref, kim_ref, kl_ref, cm_ref, sm_ref, o_ref,
                  sig_ref, gate_ref, u_ref, p_ref):
    seq_len = v_ref.shape[1]
    nchunk = seq_len // HY_ROWS
    row = lax.broadcasted_iota(jnp.int32, (seq_len, 1), 0)
    sgn = (1 - 2 * (row & 1)).astype(F32)

    def long_conv(order, epilogue):
        sig = sig_ref[...]
        sig_b = sig.astype(BF16)
        nyq = jnp.sum(sig * sgn, axis=0, keepdims=True) * kl_ref[order]
        for ch in range(nchunk):
            rows = slice(ch * HY_ROWS, (ch + 1) * HY_ROWS)
            re = _dot(cm_ref[rows, :], sig_b)
            im = _dot(sm_ref[rows, :], sig_b)
            kre = kre_ref[order, rows, :].astype(F32)
            kim = kim_ref[order, rows, :].astype(F32)
            u_ref[rows, :] = (re * kre + im * kim).astype(BF16)
            p_ref[rows, :] = (im * kre - re * kim).astype(BF16)
        for ch in range(nchunk):
            rows = slice(ch * HY_ROWS, (ch + 1) * HY_ROWS)
            y = _dot(cm_ref[rows, :], u_ref[...]) + _dot(sm_ref[rows, :], p_ref[...])
            epilogue(rows, y + sgn[rows] * nyq)

    sig_ref[...] = _short_conv(v_ref[0], wv_ref[...], bv_ref[...], seq_len)
    gate_ref[...] = _short_conv(x1_ref[0], w1_ref[...], b1_ref[...], seq_len)

    def first(rows, y):
        sig = sig_ref[rows, :]
        sig_ref[rows, :] = gate_ref[rows, :] * (y + skip_ref[0:1, :] * sig)

    long_conv(0, first)
    gate_ref[...] = _short_conv(x2_ref[0], w2_ref[...], b2_ref[...], seq_len)

    def second(rows, y):
        o_ref[0, rows, :] = (gate_ref[rows, :] * (y + skip_ref[1:2, :] * sig_ref[rows, :])).astype(o_ref.dtype)

    long_conv(1, second)


def _hyena(hy, conv_w, conv_b, skip, kre, kim, kl, cm, sm):
    bsz, seq_len, _ = hy.shape
    c = skip.shape[1]
    tc = HY_CTILE
    nct = c // tc

    def xspec(part):
        return pl.BlockSpec((1, seq_len, tc), lambda j, b: (b, 0, part * nct + j))

    def wspec(part):
        return pl.BlockSpec((HY_SHORT, tc), lambda j, b: (0, part * nct + j))

    def bspec(part):
        return pl.BlockSpec((1, tc), lambda j, b: (0, part * nct + j))

    const2 = lambda j, b: (0, 0)
    return pl.pallas_call(
        _hyena_kernel,
        grid=(nct, bsz),
        in_specs=[xspec(0), xspec(1), xspec(2), wspec(0), wspec(1), wspec(2),
                  bspec(0), bspec(1), bspec(2),
                  pl.BlockSpec((HY_ORDER, tc), lambda j, b: (0, j)),
                  pl.BlockSpec((HY_ORDER, seq_len, tc), lambda j, b: (0, 0, j)),
                  pl.BlockSpec((HY_ORDER, seq_len, tc), lambda j, b: (0, 0, j)),
                  pl.BlockSpec((HY_ORDER, 1, tc), lambda j, b: (0, 0, j)),
                  pl.BlockSpec((seq_len, seq_len), const2, pipeline_mode=pl.Buffered(1)),
                  pl.BlockSpec((seq_len, seq_len), const2, pipeline_mode=pl.Buffered(1))],
        out_specs=pl.BlockSpec((1, seq_len, tc), lambda j, b: (b, 0, j)),
        out_shape=jax.ShapeDtypeStruct((bsz, seq_len, c), BF16),
        scratch_shapes=[pltpu.VMEM((seq_len, tc), F32), pltpu.VMEM((seq_len, tc), F32),
                        pltpu.VMEM((seq_len, tc), BF16), pltpu.VMEM((seq_len, tc), BF16)],
        compiler_params=_cparams(("parallel", "arbitrary")),
        name="hyena_conv",
    )(hy, hy, hy, conv_w, conv_w, conv_w, conv_b, conv_b, conv_b, skip, kre, kim, kl, cm, sm)


def _mlstm_kernel(q_ref, k_ref, v_ref, og_ref, kt_ref, gc_ref, gr_ref, bc_ref, br_ref, nw_ref,
                  o_ref, hf_ref, hb_ref, *, chunk, nchunks, dk, dv):
    rowi = lax.broadcasted_iota(jnp.int32, (chunk, chunk), 0)
    coli = lax.broadcasted_iota(jnp.int32, (chunk, chunk), 1)
    lower = coli <= rowi
    upper = coli >= rowi
    bias_c = bc_ref[0]
    bias_r = br_ref[0]
    scale = dk ** -0.5
    neg_inf = -jnp.inf

    def step(c, direction, state):
        c_st, n_st, m_st = state
        r0 = pl.multiple_of(c * chunk, chunk)
        gcol = gc_ref[0, 0, pl.ds(r0, chunk), :] + bias_c
        grow = gr_ref[0, 0, c] + bias_r
        gi, gf = 2 * direction, 2 * direction + 1
        i_col = gcol[:, gi:gi + 1]
        f_col = jax.nn.log_sigmoid(gcol[:, gf:gf + 1])
        i_row = grow[gi:gi + 1, :]
        f_row = jax.nn.log_sigmoid(grow[gf:gf + 1, :])
        mask, mask_t = (lower, upper) if direction == 0 else (upper, lower)
        b_col = jnp.sum(jnp.where(mask, f_row, 0.0), axis=1, keepdims=True)
        b_row = jnp.sum(jnp.where(mask_t, f_col, 0.0), axis=0, keepdims=True)
        g = jnp.sum(f_row, axis=1, keepdims=True)
        dmat = jnp.where(mask, b_col - b_row + i_row, neg_inf)
        inter = b_col + m_st
        m_j = jnp.maximum(inter, jnp.max(dmat, axis=1, keepdims=True))
        w_intra = jnp.exp(dmat - m_j)
        w_inter = jnp.exp(inter - m_j)
        qc = q_ref[0, pl.ds(r0, chunk), :] * scale
        qb = qc.astype(BF16)
        kt = kt_ref[0, 0, c].astype(BF16)
        kc = k_ref[0, pl.ds(r0, chunk), :]
        vc = v_ref[0, pl.ds(r0, chunk), :]
        s = _dot(qb, kt) * w_intra
        num = w_inter * _dot(qb, c_st.astype(BF16)) + _dot(s.astype(BF16), vc.astype(BF16))
        den = (w_inter * jnp.sum(qc * n_st, axis=1, keepdims=True)
               + jnp.sum(s, axis=1, keepdims=True))
        h = num / jnp.maximum(jnp.abs(den), jnp.exp(-m_j))
        lw = g - b_col + i_col
        m_new = jnp.maximum(g + m_st, jnp.max(lw, axis=0, keepdims=True))
        wl = jnp.exp(lw - m_new)
        decay = jnp.exp(g + m_st - m_new)
        c_new = decay * c_st + _dot(kt, (wl * vc).astype(BF16))
        n_new = decay * n_st + jnp.sum(wl * kc, axis=0, keepdims=True)
        return h, (c_new, n_new, m_new)

    def body(c, carry):
        st_f, st_b = carry
        h_f, st_f = step(c, 0, st_f)
        hf_ref[pl.ds(pl.multiple_of(c * chunk, chunk), chunk), :] = h_f
        cb = nchunks - 1 - c
        h_b, st_b = step(cb, 1, st_b)
        hb_ref[pl.ds(pl.multiple_of(cb * chunk, chunk), chunk), :] = h_b
        return st_f, st_b

    zero = (jnp.zeros((dk, dv), F32), jnp.zeros((1, dk), F32), jnp.zeros((1, 1), F32))
    lax.fori_loop(0, nchunks, body, (zero, zero))

    hsum = hf_ref[...] + hb_ref[...]
    hn = hsum * lax.rsqrt(jnp.mean(hsum * hsum, axis=-1, keepdims=True) + EPS) * nw_ref[...]
    o_ref[0] = (jax.nn.sigmoid(og_ref[0]) * hn).astype(o_ref.dtype)


def _mlstm(ml, col0, gates, gate_b, norm_w, job):
    bsz, seq_len, _ = ml.shape
    nh = ML_HEADS
    dv = norm_w.shape[0] // nh
    dk = dv // 2
    chunk = min(ML_CHUNK, seq_len)
    nchunks = seq_len // chunk
    koff = col0 + nh * dk
    qb0, vb0 = col0 // dk, (col0 + 2 * nh * dk) // dv
    k_t = (ml[:, :, koff:koff + nh * dk].reshape(bsz, nchunks, chunk, nh, dk)
           .transpose(0, 3, 1, 4, 2))
    g4 = gates.reshape(bsz, seq_len, 4, nh)
    g_col = g4.transpose(0, 3, 1, 2)
    g_row = g4.reshape(bsz, nchunks, chunk, 4, nh).transpose(0, 4, 1, 3, 2)
    gb = gate_b.astype(F32).reshape(4, nh)
    b_col = gb.T.reshape(nh, 1, 4)
    b_row = gb.T.reshape(nh, 4, 1)
    kern = functools.partial(_mlstm_kernel, chunk=chunk, nchunks=nchunks, dk=dk, dv=dv)
    return _call_with_cast(
        kern, job,
        (ml, ml, ml, ml, k_t, g_col, g_row, b_col, b_row, norm_w.reshape(1, nh * dv).astype(F32)),
        grid=(bsz, nh),
        in_specs=[pl.BlockSpec((1, seq_len, dk), lambda b, h: (b, 0, qb0 + h)),
                  pl.BlockSpec((1, seq_len, dk), lambda b, h: (b, 0, qb0 + nh + h)),
                  pl.BlockSpec((1, seq_len, dv), lambda b, h: (b, 0, vb0 + h)),
                  pl.BlockSpec((1, seq_len, dv), lambda b, h: (b, 0, vb0 + nh + h)),
                  pl.BlockSpec((1, 1, nchunks, dk, chunk), lambda b, h: (b, h, 0, 0, 0)),
                  pl.BlockSpec((1, 1, seq_len, 4), lambda b, h: (b, h, 0, 0)),
                  pl.BlockSpec((1, 1, nchunks, 4, chunk), lambda b, h: (b, h, 0, 0, 0)),
                  pl.BlockSpec((1, 1, 4), lambda b, h: (h, 0, 0)),
                  pl.BlockSpec((1, 4, 1), lambda b, h: (h, 0, 0)),
                  pl.BlockSpec((1, dv), lambda b, h: (0, h))],
        out_specs=pl.BlockSpec((1, seq_len, dv), lambda b, h: (b, 0, h)),
        out_shape=jax.ShapeDtypeStruct((bsz, seq_len, nh * dv), BF16),
        scratch_shapes=[pltpu.VMEM((seq_len, dv), F32), pltpu.VMEM((seq_len, dv), F32)],
        sem=("arbitrary", "arbitrary"),
        name="mlstm",
    )


def _rglru_kernel(x_ref, y_ref, cw_ref, cb_ref, wa_ref, wx_ref, ba_ref, bx_ref, lam_ref,
                  o_ref, a_ref, b_ref, h_ref):
    seq_len, width = x_ref.shape[1], x_ref.shape[2]
    nblk = seq_len // SUBLANE
    x = x_ref[0]
    row = lax.broadcasted_iota(jnp.int32, (seq_len, 1), 0)
    cw = cw_ref[...]
    xc = (jnp.where(row < 2, 0.0, pltpu.roll(x, 2, 0)) * cw[0:1]
          + jnp.where(row < 1, 0.0, pltpu.roll(x, 1, 0)) * cw[1:2]
          + x * cw[2:3]
          + jnp.where(row == seq_len - 1, 0.0, pltpu.roll(x, seq_len - 1, 0)) * cw[3:4]
          + cb_ref[...])
    xb = xc.astype(BF16)
    row8 = lax.broadcasted_iota(jnp.int32, (SUBLANE, width), 0)

    for direction in range(2):
        r = jax.nn.sigmoid(_dot(xb, wa_ref[direction, 0]) + ba_ref[direction:direction + 1, :])
        ig = jax.nn.sigmoid(_dot(xb, wx_ref[direction, 0]) + bx_ref[direction:direction + 1, :])
        log_a = -RG_C * r * jax.nn.softplus(-lam_ref[direction:direction + 1, :])
        a_ref[...] = jnp.exp(log_a)
        th = jnp.tanh(log_a)
        b_ref[...] = jnp.sqrt(-2.0 * th / (1.0 - th)) * (ig * xc)

        def block(i, carry, direction=direction):
            blk = i if direction == 0 else nblk - 1 - i
            r0 = pl.multiple_of(blk * SUBLANE, SUBLANE)
            a = a_ref[pl.ds(r0, SUBLANE), :]
            b = b_ref[pl.ds(r0, SUBLANE), :]
            for d in (1, 2, 4):
                if direction == 0:
                    shift, valid = d, row8 >= d
                else:
                    shift, valid = SUBLANE - d, row8 < SUBLANE - d
                a_s = jnp.where(valid, pltpu.roll(a, shift, 0), 1.0)
                b_s = jnp.where(valid, pltpu.roll(b, shift, 0), 0.0)
                b = a * b_s + b
                a = a * a_s
            h = b + a * carry
            if direction == 0:
                h_ref[pl.ds(r0, SUBLANE), :] = h
                return h[SUBLANE - 1:SUBLANE, :]
            h_ref[pl.ds(r0, SUBLANE), :] += h
            return h[0:1, :]

        lax.fori_loop(0, nblk, block, jnp.zeros((1, width), F32), unroll=4)

    o_ref[0] = (h_ref[...] * jax.nn.gelu(y_ref[0])).astype(o_ref.dtype)


def _rglru(rg, conv_w, conv_b, wa, ba, wx, bx, lam, job):
    bsz, seq_len, w2 = rg.shape
    width = w2 // 2
    nh = RG_HEADS
    blk = width // nh
    wspec = pl.BlockSpec((2, 1, blk, blk), lambda b, h: (0, h, 0, 0))
    vspec = pl.BlockSpec((2, blk), lambda b, h: (0, h))
    return _call_with_cast(
        _rglru_kernel, job,
        (rg, rg, conv_w.astype(F32), conv_b.reshape(1, width).astype(F32),
         wa.astype(BF16), wx.astype(BF16), ba.astype(F32), bx.astype(F32), lam.astype(F32)),
        grid=(bsz, nh),
        in_specs=[pl.BlockSpec((1, seq_len, blk), lambda b, h: (b, 0, h)),
                  pl.BlockSpec((1, seq_len, blk), lambda b, h: (b, 0, nh + h)),
                  pl.BlockSpec((RG_CONV, blk), lambda b, h: (0, h)),
                  pl.BlockSpec((1, blk), lambda b, h: (0, h)),
                  wspec, wspec, vspec, vspec, vspec],
        out_specs=pl.BlockSpec((1, seq_len, blk), lambda b, h: (b, 0, h)),
        out_shape=jax.ShapeDtypeStruct((bsz, seq_len, width), BF16),
        scratch_shapes=[pltpu.VMEM((seq_len, blk), F32)] * 3,
        sem=("arbitrary", "arbitrary"),
        name="rglru",
    )


def _pad_cols(w, n):
    return jnp.pad(w, ((0, 0), (0, n - w.shape[1])))


def kernel(x, mix_norm, w_in, b_gate, hy_conv_w, hy_conv_b, hy_w1, hy_b1, hy_w2, hy_b2, hy_freq, hy_w3, hy_skip, ml_gate_b, ml_norm, rg_conv_w, rg_conv_b, rg_wa, rg_ba, rg_wx, rg_bx, rg_lambda, w_br_a, w_br_b, w_br_c, w_out, ffn_norm, w_gate, w_up, w_down, final_norm):
    bsz, seq_len, d_model = x.shape
    depth = mix_norm.shape[0]
    m = bsz * seq_len
    d_mix = d_model // 2
    dv = d_mix // ML_HEADS
    dk = dv // 2
    n_hy = (HY_ORDER + 1) * d_mix
    n_ml = 2 * ML_HEADS * dk + 2 * ML_HEADS * dv
    n_gt = 4 * ML_HEADS
    n_rg = 2 * d_mix
    o_ml = n_hy
    o_gt = o_ml + n_ml
    o_rg = o_gt + n_gt
    o_g = o_rg + n_rg
    ffn_hidden = w_gate.shape[2]
    ffn_pad = -(-ffn_hidden // FFN_PAD) * FFN_PAD

    cm, sm = _dft_tables(seq_len)
    t_pos, feat = _filter_features(seq_len)
    feat_pad = _pad_cols(feat, LANE)
    deltas = jnp.abs(jnp.linspace(math.log(HY_DECAY_TARGET) / HY_SLOW_DECAY,
                                  math.log(HY_DECAY_TARGET) / HY_FAST_DECAY,
                                  d_mix, dtype=F32)).reshape(1, d_mix)

    w_in_t = jnp.swapaxes(w_in, 1, 2)
    h = x.reshape(m, d_model).astype(F32)
    for l in range(depth):
        u = _rmsnorm(h, mix_norm[l], BF16)
        hm = _matmul(u, _cast_weight_t(w_in_t, l, 0, o_gt), F32, "in_proj_hyena_mlstm")
        hm = hm.reshape(bsz, seq_len, o_gt)
        gt = _matmul(u, _cast_weight_t(w_in_t, l, o_gt, n_gt), F32, "in_proj_mlstm_gates")
        w_tail = _cast_weight_t(w_in_t, l, o_rg, n_rg + N_BRANCH * d_model)
        rg = _matmul(u, w_tail, F32, "in_proj_rglru", 0, n_rg)
        gate = _matmul_sigmoid(u, w_tail, b_gate[l].reshape(1, N_BRANCH * d_model).astype(F32),
                               "in_proj_merge_gates", n_rg)

        w1_pad = jnp.pad(hy_w1[l].astype(F32), ((0, LANE - HY_EMB), (0, 0)))
        hdn = _filter_mlp(feat_pad, w1_pad, hy_b1[l].astype(F32), hy_w2[l].astype(F32),
                          hy_b2[l].astype(F32), hy_freq[l].astype(F32))
        kre, kim, kl = _filter_spectra(hdn, hy_w3[l].astype(F32), t_pos.reshape(seq_len, 1), deltas, cm, sm)
        y_a = _hyena(hm, hy_conv_w[l].astype(F32),
                     hy_conv_b[l].reshape(1, n_hy).astype(F32), hy_skip[l].astype(F32),
                     kre, kim, kl, cm, sm)

        mixer_steps = bsz * ML_HEADS
        y_b, w_g = _mlstm(hm, o_ml, gt.reshape(bsz, seq_len, n_gt), ml_gate_b[l], ml_norm[l],
                          _cast_job(w_gate, l, 1, 2 * LANE, ffn_pad, mixer_steps))
        y_c, w_u = _rglru(rg.reshape(bsz, seq_len, n_rg), rg_conv_w[l], rg_conv_b[l],
                          rg_wa[l], rg_ba[l], rg_wx[l], rg_bx[l], rg_lambda[l],
                          _cast_job(w_up, l, 1, 2 * LANE, ffn_pad, mixer_steps))

        merged = _merge(y_a.reshape(m, d_mix), y_b.reshape(m, d_mix), y_c.reshape(m, d_mix),
                        _cast_weight(w_br_a, l), _cast_weight(w_br_b, l), _cast_weight(w_br_c, l), gate)
        h = _matmul_residual_fullk(merged, _cast_weight(w_out, l), h, "out_proj")

        u = _rmsnorm(h, ffn_norm[l], BF16)
        gu_steps = (m // min(MM_TILE, m)) * (ffn_pad // (MM_TILE // 2))
        ff, w_d = _ffn_gate_up(u, w_g, w_u, _cast_job(w_down, l, 0, LANE, ffn_pad, gu_steps))
        h = _matmul_residual(ff, w_d, h, ffn_pad // 4, "ffn_down")

    out = _rmsnorm(h, final_norm, x.dtype)
    return out.reshape(bsz, seq_len, d_model)
```

```python
import functools
import math
from typing import NamedTuple

import jax
import jax.numpy as jnp
from jax import lax
from jax.experimental import pallas as pl
from jax.experimental.pallas import tpu as pltpu

F32 = jnp.float32
BF16 = jnp.bfloat16

N_BRANCH = 3
HY_ORDER = 2
HY_SHORT = 3
HY_EMB = 33
HY_BANDS = (HY_EMB - 1) // 2
HY_FILTER_HIDDEN = 64
HY_N_INNER = 2
HY_FAST_DECAY = 0.3
HY_SLOW_DECAY = 1.5
HY_DECAY_TARGET = 1e-2
ML_HEADS = 8
RG_HEADS = 8
RG_CONV = 4
RG_C = 8.0
EPS = 1e-6

V7X_VMEM_LIMIT_BYTES = 56 * 1024 * 1024
LANE = 128
SUBLANE = 8
MM_TILE = 1024
ML_CHUNK = 256
HY_CTILE = 256
HY_ROWS = 512
FFN_PAD = 1024
CAST_BLOCK_ELEMS = 1024 * 1024
DFT_SPLIT = 64


def _cparams(sem, vmem=V7X_VMEM_LIMIT_BYTES):
    return pltpu.CompilerParams(dimension_semantics=sem, vmem_limit_bytes=vmem)


def _dot(a, b):
    return jnp.dot(a, b, preferred_element_type=F32)


def _dot_hi(a, b):
    return jnp.dot(a, b, preferred_element_type=F32, precision=lax.Precision.HIGHEST)


def _cast_kernel(x_ref, o_ref, *, src_row_blocks, src_col_blocks):
    inside = (pl.program_id(0) < src_row_blocks) & (pl.program_id(1) < src_col_blocks)
    o_ref[...] = jnp.where(inside, x_ref[...], 0.0).astype(o_ref.dtype)


def _cast_weight(w_stack, layer, col0=0, ncols=None, out_rows=None, out_cols=None):
    _, k, n = w_stack.shape
    ncols = n - col0 if ncols is None else ncols
    out_rows = k if out_rows is None else out_rows
    out_cols = ncols if out_cols is None else out_cols
    tc = next(t for t in (4096, 2048, 1024, 512, 256, LANE)
              if ncols % t == 0 and out_cols % t == 0 and col0 % t == 0)
    tr = next(t for t in (2048, 1024, 512, 256)
              if k % t == 0 and out_rows % t == 0 and t * tc <= CAST_BLOCK_ELEMS)
    src_rb, src_cb, cb0 = k // tr, ncols // tc, col0 // tc
    kern = functools.partial(_cast_kernel, src_row_blocks=src_rb, src_col_blocks=src_cb)
    return pl.pallas_call(
        kern,
        grid=(out_rows // tr, out_cols // tc),
        in_specs=[pl.BlockSpec((None, tr, tc), lambda i, j: (layer, jnp.minimum(i, src_rb - 1),
                                                               cb0 + jnp.minimum(j, src_cb - 1)))],
        out_specs=pl.BlockSpec((tr, tc), lambda i, j: (i, j)),
        out_shape=jax.ShapeDtypeStruct((out_rows, out_cols), BF16),
        compiler_params=_cparams(("parallel", "parallel")),
        name="weight_cast",
    )(w_stack)


class _CastJob(NamedTuple):
    src: jax.Array
    layer: int
    axis: int
    block: tuple
    n_src: int
    n_out: int
    group: int

    @property
    def steps(self):
        return self.n_out // self.group

    @property
    def out_block(self):
        return tuple(d * self.group if a == self.axis else d for a, d in enumerate(self.block))

    @property
    def out_shape(self):
        return tuple(d * self.n_out if a == self.axis else d for a, d in enumerate(self.block))


def _cast_job(src, layer, axis, width, padded, host_steps):
    size = src.shape[1 + axis]
    assert size % width == 0 and padded % width == 0
    n_out = padded // width
    group = next(g for g in range(1, n_out + 1) if n_out % g == 0 and n_out // g <= host_steps)
    block = (width, src.shape[2]) if axis == 0 else (src.shape[1], width)
    return _CastJob(src, layer, axis, block, size // width, n_out, group)


def _linear_step(ids, grid):
    t = ids[0]
    for idx, n in zip(ids[1:], grid[1:]):
        t = t * n + idx
    return t


def _call_with_cast(body, job, args, *, grid, in_specs, out_specs, out_shape, scratch_shapes, sem, name):
    n_in = len(in_specs)
    assert job.steps <= math.prod(grid) and job.n_out % job.group == 0

    def local(ids):
        return jnp.minimum(_linear_step(ids, grid), job.steps - 1)

    def src_map(q):
        def index(*ids):
            blk = jnp.minimum(local(ids) * job.group + q, job.n_src - 1)
            return (job.layer, blk, 0) if job.axis == 0 else (job.layer, 0, blk)
        return index

    def dst_map(*ids):
        return (local(ids), 0) if job.axis == 0 else (0, local(ids))

    def kern(*refs):
        srcs = refs[n_in:n_in + job.group]
        host_out, dst = refs[n_in + job.group], refs[n_in + job.group + 1]
        body(*refs[:n_in], host_out, *refs[n_in + job.group + 2:])
        t = _linear_step([pl.program_id(a) for a in range(len(grid))], grid)

        @pl.when(t < job.steps)
        def _():
            parts = [jnp.where(t * job.group + q < job.n_src, src[...], 0.0) for q, src in enumerate(srcs)]
            dst[...] = jnp.concatenate(parts, axis=job.axis).astype(dst.dtype)

    src_specs = [pl.BlockSpec((None,) + tuple(job.block), src_map(q)) for q in range(job.group)]
    return pl.pallas_call(
        kern,
        grid=grid,
        in_specs=list(in_specs) + src_specs,
        out_specs=[out_specs, pl.BlockSpec(job.out_block, dst_map)],
        out_shape=[out_shape, jax.ShapeDtypeStruct(job.out_shape, BF16)],
        scratch_shapes=scratch_shapes,
        compiler_params=_cparams(sem),
        name=name,
    )(*args, *([job.src] * job.group))


def _cast_t_kernel(a_ref, o_ref):
    o_ref[...] = a_ref[...].T.astype(o_ref.dtype)


def _cast_t_shift_kernel(a_ref, b_ref, o_ref, *, shift):
    rows = jnp.concatenate([a_ref[shift:, :], b_ref[...]], axis=0)
    o_ref[...] = rows.T.astype(o_ref.dtype)


def _cast_weight_t(w_t, layer, row0, nrows):
    _, _, k = w_t.shape
    tr = min(CAST_BLOCK_ELEMS // k, nrows)
    shift = row0 % tr
    base = row0 - shift
    out_spec = pl.BlockSpec((k, tr), lambda i: (0, i))
    a_spec = pl.BlockSpec((None, tr, k), lambda i: (layer, base // tr + i, 0))
    common = dict(grid=(nrows // tr,), out_specs=out_spec,
                  out_shape=jax.ShapeDtypeStruct((k, nrows), BF16),
                  compiler_params=_cparams(("parallel",)))
    if shift == 0:
        return pl.pallas_call(_cast_t_kernel, in_specs=[a_spec], name="weight_cast_t", **common)(w_t)
    b_spec = pl.BlockSpec((None, shift, k), lambda i: (layer, (base + (i + 1) * tr) // shift, 0))
    kern = functools.partial(_cast_t_shift_kernel, shift=shift)
    return pl.pallas_call(kern, in_specs=[a_spec, b_spec], name="weight_cast_t_shifted", **common)(w_t, w_t)


def _rms_kernel(x_ref, g_ref, o_ref):
    x = x_ref[...]
    ms = jnp.mean(x * x, axis=-1, keepdims=True)
    o_ref[...] = (x * lax.rsqrt(ms + EPS) * g_ref[...]).astype(o_ref.dtype)


def _rmsnorm(x2d, g, out_dtype):
    m, d = x2d.shape
    tm = min(256, m)
    return pl.pallas_call(
        _rms_kernel,
        grid=(m // tm,),
        in_specs=[pl.BlockSpec((tm, d), lambda i: (i, 0)),
                  pl.BlockSpec((1, d), lambda i: (0, 0))],
        out_specs=pl.BlockSpec((tm, d), lambda i: (i, 0)),
        out_shape=jax.ShapeDtypeStruct((m, d), out_dtype),
        compiler_params=_cparams(("parallel",)),
        name="rmsnorm",
    )(x2d, g.reshape(1, d).astype(F32))


def _mm_kernel(a_ref, w_ref, o_ref):
    o_ref[...] = _dot(a_ref[...], w_ref[...]).astype(o_ref.dtype)


def _matmul(a, w, out_dtype, name, col0=0, ncols=None):
    m, k = a.shape
    n = w.shape[1] - col0 if ncols is None else ncols
    tm = min(MM_TILE, m)
    tn = min(MM_TILE, n)
    cb0 = col0 // tn
    return pl.pallas_call(
        _mm_kernel,
        grid=(m // tm, n // tn),
        in_specs=[pl.BlockSpec((tm, k), lambda i, j: (i, 0)),
                  pl.BlockSpec((k, tn), lambda i, j: (0, cb0 + j))],
        out_specs=pl.BlockSpec((tm, tn), lambda i, j: (i, j)),
        out_shape=jax.ShapeDtypeStruct((m, n), out_dtype),
        compiler_params=_cparams(("parallel", "parallel")),
        name=name,
    )(a, w)


def _mm_sigmoid_kernel(a_ref, w_ref, b_ref, o_ref):
    o_ref[...] = jax.nn.sigmoid(_dot(a_ref[...], w_ref[...]) + b_ref[...]).astype(o_ref.dtype)


def _matmul_sigmoid(a, w, bias, name, col0=0):
    m, k = a.shape
    n = w.shape[1] - col0
    tm = min(MM_TILE, m)
    tn = min(MM_TILE, n)
    cb0 = col0 // tn
    return pl.pallas_call(
        _mm_sigmoid_kernel,
        grid=(m // tm, n // tn),
        in_specs=[pl.BlockSpec((tm, k), lambda i, j: (i, 0)),
                  pl.BlockSpec((k, tn), lambda i, j: (0, cb0 + j)),
                  pl.BlockSpec((1, tn), lambda i, j: (0, j))],
        out_specs=pl.BlockSpec((tm, tn), lambda i, j: (i, j)),
        out_shape=jax.ShapeDtypeStruct((m, n), BF16),
        compiler_params=_cparams(("parallel", "parallel")),
        name=name,
    )(a, w, bias)


def _mm_res_kernel(a_ref, w_ref, r_ref, o_ref):
    k = pl.program_id(2)

    @pl.when(k == 0)
    def _():
        o_ref[...] = r_ref[...]

    o_ref[...] += _dot(a_ref[...], w_ref[...])


def _matmul_residual(a, w, res, tk, name):
    m, k = a.shape
    n = w.shape[1]
    tm = min(MM_TILE, m)
    tn = min(MM_TILE, n)
    return pl.pallas_call(
        _mm_res_kernel,
        grid=(m // tm, n // tn, k // tk),
        in_specs=[pl.BlockSpec((tm, tk), lambda i, j, kk: (i, kk)),
                  pl.BlockSpec((tk, tn), lambda i, j, kk: (kk, j)),
                  pl.BlockSpec((tm, tn), lambda i, j, kk: (i, j))],
        out_specs=pl.BlockSpec((tm, tn), lambda i, j, kk: (i, j)),
        out_shape=jax.ShapeDtypeStruct((m, n), F32),
        compiler_params=_cparams(("parallel", "parallel", "arbitrary")),
        name=name,
    )(a, w, res)


def _mm_res_fullk_kernel(a_ref, w_ref, r_ref, o_ref):
    o_ref[...] = r_ref[...] + _dot(a_ref[...], w_ref[...])


def _matmul_residual_fullk(a, w, res, name):
    m, k = a.shape
    n = w.shape[1]
    tm = min(MM_TILE, m)
    tn = min(MM_TILE // 2, n)
    return pl.pallas_call(
        _mm_res_fullk_kernel,
        grid=(m // tm, n // tn),
        in_specs=[pl.BlockSpec((tm, k), lambda i, j: (i, 0)),
                  pl.BlockSpec((k, tn), lambda i, j: (0, j)),
                  pl.BlockSpec((tm, tn), lambda i, j: (i, j))],
        out_specs=pl.BlockSpec((tm, tn), lambda i, j: (i, j)),
        out_shape=jax.ShapeDtypeStruct((m, n), F32),
        compiler_params=_cparams(("parallel", "parallel")),
        name=name,
    )(a, w, res)


def _ffn_gu_kernel(a_ref, wg_ref, wu_ref, o_ref):
    a = a_ref[...]
    g = _dot(a, wg_ref[...])
    u = _dot(a, wu_ref[...])
    o_ref[...] = (g * jax.nn.sigmoid(g) * u).astype(o_ref.dtype)


def _ffn_gate_up(a, wg, wu, job):
    m, k = a.shape
    n = wg.shape[1]
    tm = min(MM_TILE, m)
    tn = MM_TILE // 2
    return _call_with_cast(
        _ffn_gu_kernel, job, (a, wg, wu),
        grid=(m // tm, n // tn),
        in_specs=[pl.BlockSpec((tm, k), lambda i, j: (i, 0)),
                  pl.BlockSpec((k, tn), lambda i, j: (0, j)),
                  pl.BlockSpec((k, tn), lambda i, j: (0, j))],
        out_specs=pl.BlockSpec((tm, tn), lambda i, j: (i, j)),
        out_shape=jax.ShapeDtypeStruct((m, n), BF16),
        scratch_shapes=[],
        sem=("arbitrary", "arbitrary"),
        name="ffn_gate_up",
    )


def _merge_kernel(ya_ref, yb_ref, yc_ref, wa_ref, wb_ref, wc_ref, ga_ref, gb_ref, gc_ref, o_ref):
    acc = ga_ref[...].astype(F32) * _dot(ya_ref[...], wa_ref[...])
    acc += gb_ref[...].astype(F32) * _dot(yb_ref[...], wb_ref[...])
    acc += gc_ref[...].astype(F32) * _dot(yc_ref[...], wc_ref[...])
    o_ref[...] = acc.astype(o_ref.dtype)


def _merge(ya, yb, yc, wa, wb, wc, gate):
    m, c = ya.shape
    d = wa.shape[1]
    tm = min(MM_TILE, m)
    tn = MM_TILE // 2
    nb = d // tn
    yspec = pl.BlockSpec((tm, c), lambda i, j: (i, 0))
    wspec = pl.BlockSpec((c, tn), lambda i, j: (0, j))

    def gspec(br):
        return pl.BlockSpec((tm, tn), lambda i, j: (i, br * nb + j))

    return pl.pallas_call(
        _merge_kernel,
        grid=(m // tm, nb),
        in_specs=[yspec, yspec, yspec, wspec, wspec, wspec, gspec(0), gspec(1), gspec(2)],
        out_specs=pl.BlockSpec((tm, tn), lambda i, j: (i, j)),
        out_shape=jax.ShapeDtypeStruct((m, d), BF16),
        compiler_params=_cparams(("parallel", "parallel")),
        name="branch_merge",
    )(ya, yb, yc, wa, wb, wc, gate, gate, gate)


def _dft_tables(seq_len):
    f = jnp.arange(seq_len, dtype=jnp.int32)

    def trig(s):
        ang = ((f[:, None] * s[None, :]) % (2 * seq_len)).astype(F32) * (math.pi / seq_len)
        return jnp.cos(ang), jnp.sin(ang)

    c0, s0 = trig(jnp.arange(DFT_SPLIT, dtype=jnp.int32))
    c1, s1 = trig(jnp.arange(seq_len // DFT_SPLIT, dtype=jnp.int32) * DFT_SPLIT)
    cm = c1[:, :, None] * c0[:, None, :] - s1[:, :, None] * s0[:, None, :]
    sm = s1[:, :, None] * c0[:, None, :] + c1[:, :, None] * s0[:, None, :]
    return cm.reshape(seq_len, seq_len).astype(BF16), sm.reshape(seq_len, seq_len).astype(BF16)


def _filter_features(seq_len):
    pos = jnp.arange(seq_len, dtype=F32)
    t = pos / (seq_len - 1)
    omega = 2.0 * math.pi * pos / seq_len
    bands = jnp.linspace(1e-4, HY_BANDS - 1, HY_BANDS, dtype=F32)
    ang = omega[:, None] * bands[None, :]
    feat = jnp.concatenate([t[:, None], jnp.cos(ang), -jnp.sin(ang)], axis=-1)
    return t, feat


def _filter_mlp_kernel(feat_ref, w1_ref, b1_ref, w2_ref, b2_ref, freq_ref, o_ref):
    freq = freq_ref[...]
    hdn = jnp.sin(freq * (_dot_hi(feat_ref[...], w1_ref[...]) + b1_ref[...]))
    for j in range(HY_N_INNER):
        hdn = jnp.sin(freq * (_dot_hi(hdn, w2_ref[j]) + b2_ref[j]))
    o_ref[...] = hdn


def _filter_mlp(feat_pad, w1_pad, b1, w2, b2, freq):
    seq_len = feat_pad.shape[0]
    hid = w1_pad.shape[1]
    return pl.pallas_call(
        _filter_mlp_kernel,
        out_shape=jax.ShapeDtypeStruct((seq_len, hid), F32),
        name="hyena_filter_mlp",
    )(feat_pad, w1_pad, b1.reshape(1, hid), w2, b2.reshape(HY_N_INNER, 1, hid), freq.reshape(1, hid))


def _half_spectrum(a_ref, b_ref, twc, tws, even, odd, rows):
    ae, be = _dot(a_ref[rows, :], even), _dot(b_ref[rows, :], even)
    ao, bo = _dot(a_ref[rows, :], odd), _dot(b_ref[rows, :], odd)
    oc = twc * ao - tws * bo
    os_ = twc * bo + tws * ao
    return ae + oc, ae - oc, be + os_, os_ - be


def _filter_spec_kernel(hdn_ref, wf_ref, wb_ref, t_ref, dl_ref, twc_ref, tws_ref, a_ref, b_ref,
                        k_ref, kmid_ref):
    seq_len = hdn_ref.shape[0]
    half = seq_len // 2
    hdn = hdn_ref[...]
    win = jnp.exp(-t_ref[...] * dl_ref[...])
    row = lax.broadcasted_iota(jnp.int32, (seq_len, 1), 0)
    fwd = _dot_hi(hdn, wf_ref[...]) * win
    bwd = jnp.where(row == 0, 0.0, _dot_hi(hdn, wb_ref[...]) * win)
    nrm = lax.rsqrt(jnp.sum(fwd * fwd, axis=0, keepdims=True)
                    + jnp.sum(bwd * bwd, axis=0, keepdims=True))
    sym = (fwd + bwd) * nrm
    asym = (bwd - fwd) * nrm
    hrow = lax.broadcasted_iota(jnp.int32, (half, 1), 0)
    alt = (1 - 2 * (hrow & 1)).astype(F32)
    wfreq = jnp.where(hrow == 0, 0.5 / seq_len, 1.0 / seq_len)
    rows = slice(0, half)
    twc, tws = twc_ref[...], tws_ref[...]
    kre_lo, kre_hi, _, _ = _half_spectrum(a_ref, b_ref, twc, tws, sym[:half].astype(BF16),
                                          sym[half:].astype(BF16), rows)
    _, _, kim_lo, kim_hi = _half_spectrum(a_ref, b_ref, twc, tws, asym[:half].astype(BF16),
                                          asym[half:].astype(BF16), rows)
    k_ref[0, 0] = (kre_lo * wfreq).astype(k_ref.dtype)
    k_ref[0, 1] = (kim_lo * wfreq).astype(k_ref.dtype)
    k_ref[0, 2] = (kre_hi * wfreq).astype(k_ref.dtype)
    k_ref[0, 3] = (kim_hi * wfreq).astype(k_ref.dtype)
    kmid_ref[0, 0:1, :] = jnp.sum(alt * sym[:half], axis=0, keepdims=True) * (1.0 / seq_len)
    kmid_ref[0, 1:2, :] = jnp.sum(alt * asym[half:], axis=0, keepdims=True) * (1.0 / seq_len)


def _filter_spectra(hdn, w3, t_col, deltas, twc, tws, amat, bmat):
    seq_len, hid = hdn.shape
    half = seq_len // 2
    c = deltas.shape[1]
    tc = HY_CTILE
    nct = c // tc
    const2 = lambda o, j: (0, 0)
    return pl.pallas_call(
        _filter_spec_kernel,
        grid=(HY_ORDER, nct),
        in_specs=[pl.BlockSpec((seq_len, hid), const2),
                  pl.BlockSpec((hid, tc), lambda o, j: (0, o * 2 * nct + j)),
                  pl.BlockSpec((hid, tc), lambda o, j: (0, o * 2 * nct + nct + j)),
                  pl.BlockSpec((seq_len, 1), const2),
                  pl.BlockSpec((1, tc), lambda o, j: (0, j)),
                  pl.BlockSpec((half, 1), const2),
                  pl.BlockSpec((half, 1), const2),
                  pl.BlockSpec((half, half), const2, pipeline_mode=pl.Buffered(1)),
                  pl.BlockSpec((half, half), const2, pipeline_mode=pl.Buffered(1))],
        out_specs=[pl.BlockSpec((1, 4, half, tc), lambda o, j: (o, 0, 0, j)),
                   pl.BlockSpec((1, 2, tc), lambda o, j: (o, 0, j))],
        out_shape=[jax.ShapeDtypeStruct((HY_ORDER, 4, half, c), BF16),
                   jax.ShapeDtypeStruct((HY_ORDER, 2, c), F32)],
        compiler_params=_cparams(("parallel", "parallel")),
        name="hyena_filter_spectra",
    )(hdn, w3, w3, t_col, deltas, twc, tws, amat, bmat)


def _short_conv(xe, xo, w, b, half):
    row = lax.broadcasted_iota(jnp.int32, (half, 1), 0)
    o_prev = jnp.where(row == 0, 0.0, pltpu.roll(xo, 1, 0))
    e_next = jnp.where(row == half - 1, 0.0, pltpu.roll(xe, half - 1, 0))
    ye = o_prev * w[0:1] + xe * w[1:2] + xo * w[2:3] + b
    yo = xe * w[0:1] + xo * w[1:2] + e_next * w[2:3] + b
    return ye, yo


def _hyena_kernel(x1_ref, x2_ref, v_ref, w1_ref, w2_ref, wv_ref, b1_ref, b2_ref, bv_ref,
                  skip_ref, k_ref, kmid_ref, twc_ref, tws_ref, a_ref, b_ref, o_ref,
                  sig_ref, gate_ref, ue_ref, ve_ref, uo_ref, vo_ref):
    seq_len = v_ref.shape[1]
    half = seq_len // 2
    chunk = min(HY_ROWS, half)
    nchunk = half // chunk
    hrow = lax.broadcasted_iota(jnp.int32, (half, 1), 0)
    alt = (1 - 2 * (hrow & 1)).astype(F32)

    def long_conv(order, epilogue):
        sig_e, sig_o = sig_ref[0:half, :], sig_ref[half:seq_len, :]
        e_b, o_b = sig_e.astype(BF16), sig_o.astype(BF16)
        r_mid = jnp.sum(alt * sig_e, axis=0, keepdims=True)
        p_mid = jnp.sum(alt * sig_o, axis=0, keepdims=True)
        kre_mid, kim_mid = kmid_ref[order, 0:1, :], kmid_ref[order, 1:2, :]
        u_mid = r_mid * kre_mid + p_mid * kim_mid
        v_mid = p_mid * kre_mid - r_mid * kim_mid
        for ch in range(nchunk):
            rows = slice(ch * chunk, (ch + 1) * chunk)
            twc, tws = twc_ref[rows, :], tws_ref[rows, :]
            r_lo, r_hi, p_lo, p_hi = _half_spectrum(a_ref, b_ref, twc, tws, e_b, o_b, rows)
            kre_lo = k_ref[order, 0, rows, :].astype(F32)
            kim_lo = k_ref[order, 1, rows, :].astype(F32)
            kre_hi = k_ref[order, 2, rows, :].astype(F32)
            kim_hi = k_ref[order, 3, rows, :].astype(F32)
            u_lo = r_lo * kre_lo + p_lo * kim_lo
            v_lo = p_lo * kre_lo - r_lo * kim_lo
            u_hi = r_hi * kre_hi + p_hi * kim_hi
            v_hi = p_hi * kre_hi - r_hi * kim_hi
            ue_ref[rows, :] = (u_lo + u_hi).astype(BF16)
            ve_ref[rows, :] = (v_lo - v_hi).astype(BF16)
            u_od, v_od = u_lo - u_hi, v_lo + v_hi
            uo_ref[rows, :] = (twc * u_od + tws * v_od).astype(BF16)
            vo_ref[rows, :] = (twc * v_od - tws * u_od).astype(BF16)
        for ch in range(nchunk):
            rows = slice(ch * chunk, (ch + 1) * chunk)
            y_e = _dot(a_ref[rows, :], ue_ref[...]) + _dot(b_ref[rows, :], ve_ref[...]) + alt[rows] * u_mid
            y_o = _dot(a_ref[rows, :], uo_ref[...]) + _dot(b_ref[rows, :], vo_ref[...]) + alt[rows] * v_mid
            epilogue(rows, y_e, y_o)

    def conv3(x_ref, w_ref, b_ref_, dst_ref):
        x = x_ref[0]
        ye, yo = _short_conv(x[:half], x[half:], w_ref[...], b_ref_[...], half)
        dst_ref[0:half, :] = ye
        dst_ref[half:seq_len, :] = yo

    conv3(v_ref, wv_ref, bv_ref, sig_ref)
    conv3(x1_ref, w1_ref, b1_ref, gate_ref)

    def first(rows, y_e, y_o):
        for off, y in ((0, y_e), (half, y_o)):
            r = slice(off + rows.start, off + rows.stop)
            sig_ref[r, :] = gate_ref[r, :] * (y + skip_ref[0:1, :] * sig_ref[r, :])

    long_conv(0, first)
    conv3(x2_ref, w2_ref, b2_ref, gate_ref)

    def second(rows, y_e, y_o):
        for off, y in ((0, y_e), (half, y_o)):
            r = slice(off + rows.start, off + rows.stop)
            o_ref[0, r, :] = (gate_ref[r, :] * (y + skip_ref[1:2, :] * sig_ref[r, :])).astype(o_ref.dtype)

    long_conv(1, second)


def _hyena(hy, conv_w, conv_b, skip, k, kmid, twc, tws, amat, bmat):
    bsz, seq_len, _ = hy.shape
    half = seq_len // 2
    c = skip.shape[1]
    tc = HY_CTILE
    nct = c // tc

    def xspec(part):
        return pl.BlockSpec((1, seq_len, tc), lambda j, b: (b, 0, part * nct + j))

    def wspec(part):
        return pl.BlockSpec((HY_SHORT, tc), lambda j, b: (0, part * nct + j))

    def bspec(part):
        return pl.BlockSpec((1, tc), lambda j, b: (0, part * nct + j))

    const2 = lambda j, b: (0, 0)
    return pl.pallas_call(
        _hyena_kernel,
        grid=(nct, bsz),
        in_specs=[xspec(0), xspec(1), xspec(2), wspec(0), wspec(1), wspec(2),
                  bspec(0), bspec(1), bspec(2),
                  pl.BlockSpec((HY_ORDER, tc), lambda j, b: (0, j)),
                  pl.BlockSpec((HY_ORDER, 4, half, tc), lambda j, b: (0, 0, 0, j)),
                  pl.BlockSpec((HY_ORDER, 2, tc), lambda j, b: (0, 0, j)),
                  pl.BlockSpec((half, 1), const2),
                  pl.BlockSpec((half, 1), const2),
                  pl.BlockSpec((half, half), const2, pipeline_mode=pl.Buffered(1)),
                  pl.BlockSpec((half, half), const2, pipeline_mode=pl.Buffered(1))],
        out_specs=pl.BlockSpec((1, seq_len, tc), lambda j, b: (b, 0, j)),
        out_shape=jax.ShapeDtypeStruct((bsz, seq_len, c), BF16),
        scratch_shapes=[pltpu.VMEM((seq_len, tc), F32), pltpu.VMEM((seq_len, tc), F32)]
        + [pltpu.VMEM((half, tc), BF16)] * 4,
        compiler_params=_cparams(("parallel", "arbitrary")),
        name="hyena_conv",
    )(hy, hy, hy, conv_w, conv_w, conv_w, conv_b, conv_b, conv_b, skip, k, kmid, twc, tws, amat, bmat)


def _mlstm_kernel(q_ref, k_ref, v_ref, og_ref, kt_ref, gc_ref, gr_ref, bc_ref, br_ref, nw_ref,
                  o_ref, hf_ref, hb_ref, *, chunk, nchunks, dk, dv):
    rowi = lax.broadcasted_iota(jnp.int32, (chunk, chunk), 0)
    coli = lax.broadcasted_iota(jnp.int32, (chunk, chunk), 1)
    lower = coli <= rowi
    upper = coli >= rowi
    bias_c = bc_ref[0]
    bias_r = br_ref[0]
    scale = dk ** -0.5
    neg_inf = -jnp.inf

    def step(c, direction, state):
        c_st, n_st, m_st = state
        r0 = pl.multiple_of(c * chunk, chunk)
        gcol = gc_ref[0, 0, pl.ds(r0, chunk), :] + bias_c
        grow = gr_ref[0, 0, c] + bias_r
        gi, gf = 2 * direction, 2 * direction + 1
        i_col = gcol[:, gi:gi + 1]
        f_col = jax.nn.log_sigmoid(gcol[:, gf:gf + 1])
        i_row = grow[gi:gi + 1, :]
        f_row = jax.nn.log_sigmoid(grow[gf:gf + 1, :])
        mask, mask_t = (lower, upper) if direction == 0 else (upper, lower)
        b_col = jnp.sum(jnp.where(mask, f_row, 0.0), axis=1, keepdims=True)
        b_row = jnp.sum(jnp.where(mask_t, f_col, 0.0), axis=0, keepdims=True)
        g = jnp.sum(f_row, axis=1, keepdims=True)
        dmat = jnp.where(mask, b_col - b_row + i_row, neg_inf)
        inter = b_col + m_st
        m_j = jnp.maximum(inter, jnp.max(dmat, axis=1, keepdims=True))
        w_intra = jnp.exp(dmat - m_j)
        w_inter = jnp.exp(inter - m_j)
        qc = q_ref[0, pl.ds(r0, chunk), :] * scale
        qb = qc.astype(BF16)
        kt = kt_ref[0, 0, c].astype(BF16)
        kc = k_ref[0, pl.ds(r0, chunk), :]
        vc = v_ref[0, pl.ds(r0, chunk), :]
        s = _dot(qb, kt) * w_intra
        num = w_inter * _dot(qb, c_st.astype(BF16)) + _dot(s.astype(BF16), vc.astype(BF16))
        den = (w_inter * jnp.sum(qc * n_st, axis=1, keepdims=True)
               + jnp.sum(s, axis=1, keepdims=True))
        h = num / jnp.maximum(jnp.abs(den), jnp.exp(-m_j))
        lw = g - b_col + i_col
        m_new = jnp.maximum(g + m_st, jnp.max(lw, axis=0, keepdims=True))
        wl = jnp.exp(lw - m_new)
        decay = jnp.exp(g + m_st - m_new)
        c_new = decay * c_st + _dot(kt, (wl * vc).astype(BF16))
        n_new = decay * n_st + jnp.sum(wl * kc, axis=0, keepdims=True)
        return h, (c_new, n_new, m_new)

    def body(c, carry):
        st_f, st_b = carry
        h_f, st_f = step(c, 0, st_f)
        hf_ref[pl.ds(pl.multiple_of(c * chunk, chunk), chunk), :] = h_f
        cb = nchunks - 1 - c
        h_b, st_b = step(cb, 1, st_b)
        hb_ref[pl.ds(pl.multiple_of(cb * chunk, chunk), chunk), :] = h_b
        return st_f, st_b

    zero = (jnp.zeros((dk, dv), F32), jnp.zeros((1, dk), F32), jnp.zeros((1, 1), F32))
    lax.fori_loop(0, nchunks, body, (zero, zero))

    hsum = hf_ref[...] + hb_ref[...]
    hn = hsum * lax.rsqrt(jnp.mean(hsum * hsum, axis=-1, keepdims=True) + EPS) * nw_ref[...]
    o_ref[0] = (jax.nn.sigmoid(og_ref[0]) * hn).astype(o_ref.dtype)


def _mlstm(ml, col0, gates, gate_b, norm_w, job):
    bsz, seq_len, _ = ml.shape
    nh = ML_HEADS
    dv = norm_w.shape[0] // nh
    dk = dv // 2
    chunk = min(ML_CHUNK, seq_len)
    nchunks = seq_len // chunk
    koff = col0 + nh * dk
    qb0, vb0 = col0 // dk, (col0 + 2 * nh * dk) // dv
    k_t = (ml[:, :, koff:koff + nh * dk].reshape(bsz, nchunks, chunk, nh, dk)
           .transpose(0, 3, 1, 4, 2))
    g4 = gates.reshape(bsz, seq_len, 4, nh)
    g_col = g4.transpose(0, 3, 1, 2)
    g_row = g4.reshape(bsz, nchunks, chunk, 4, nh).transpose(0, 4, 1, 3, 2)
    gb = gate_b.astype(F32).reshape(4, nh)
    b_col = gb.T.reshape(nh, 1, 4)
    b_row = gb.T.reshape(nh, 4, 1)
    kern = functools.partial(_mlstm_kernel, chunk=chunk, nchunks=nchunks, dk=dk, dv=dv)
    return _call_with_cast(
        kern, job,
        (ml, ml, ml, ml, k_t, g_col, g_row, b_col, b_row, norm_w.reshape(1, nh * dv).astype(F32)),
        grid=(bsz, nh),
        in_specs=[pl.BlockSpec((1, seq_len, dk), lambda b, h: (b, 0, qb0 + h)),
                  pl.BlockSpec((1, seq_len, dk), lambda b, h: (b, 0, qb0 + nh + h)),
                  pl.BlockSpec((1, seq_len, dv), lambda b, h: (b, 0, vb0 + h)),
                  pl.BlockSpec((1, seq_len, dv), lambda b, h: (b, 0, vb0 + nh + h)),
                  pl.BlockSpec((1, 1, nchunks, dk, chunk), lambda b, h: (b, h, 0, 0, 0)),
                  pl.BlockSpec((1, 1, seq_len, 4), lambda b, h: (b, h, 0, 0)),
                  pl.BlockSpec((1, 1, nchunks, 4, chunk), lambda b, h: (b, h, 0, 0, 0)),
                  pl.BlockSpec((1, 1, 4), lambda b, h: (h, 0, 0)),
                  pl.BlockSpec((1, 4, 1), lambda b, h: (h, 0, 0)),
                  pl.BlockSpec((1, dv), lambda b, h: (0, h))],
        out_specs=pl.BlockSpec((1, seq_len, dv), lambda b, h: (b, 0, h)),
        out_shape=jax.ShapeDtypeStruct((bsz, seq_len, nh * dv), BF16),
        scratch_shapes=[pltpu.VMEM((seq_len, dv), F32), pltpu.VMEM((seq_len, dv), F32)],
        sem=("arbitrary", "arbitrary"),
        name="mlstm",
    )


def _rglru_kernel(x_ref, y_ref, cw_ref, cb_ref, wa_ref, wx_ref, ba_ref, bx_ref, lam_ref,
                  o_ref, a_ref, b_ref, h_ref):
    seq_len, width = x_ref.shape[1], x_ref.shape[2]
    nblk = seq_len // SUBLANE
    x = x_ref[0]
    row = lax.broadcasted_iota(jnp.int32, (seq_len, 1), 0)
    cw = cw_ref[...]
    xc = (jnp.where(row < 2, 0.0, pltpu.roll(x, 2, 0)) * cw[0:1]
          + jnp.where(row < 1, 0.0, pltpu.roll(x, 1, 0)) * cw[1:2]
          + x * cw[2:3]
          + jnp.where(row == seq_len - 1, 0.0, pltpu.roll(x, seq_len - 1, 0)) * cw[3:4]
          + cb_ref[...])
    xb = xc.astype(BF16)
    row8 = lax.broadcasted_iota(jnp.int32, (SUBLANE, width), 0)

    for direction in range(2):
        r = jax.nn.sigmoid(_dot(xb, wa_ref[direction, 0]) + ba_ref[direction:direction + 1, :])
        ig = jax.nn.sigmoid(_dot(xb, wx_ref[direction, 0]) + bx_ref[direction:direction + 1, :])
        log_a = -RG_C * r * jax.nn.softplus(-lam_ref[direction:direction + 1, :])
        a_ref[...] = jnp.exp(log_a)
        th = jnp.tanh(log_a)
        b_ref[...] = jnp.sqrt(-2.0 * th / (1.0 - th)) * (ig * xc)

        def block(i, carry, direction=direction):
            blk = i if direction == 0 else nblk - 1 - i
            r0 = pl.multiple_of(blk * SUBLANE, SUBLANE)
            a = a_ref[pl.ds(r0, SUBLANE), :]
            b = b_ref[pl.ds(r0, SUBLANE), :]
            for d in (1, 2, 4):
                if direction == 0:
                    shift, valid = d, row8 >= d
                else:
                    shift, valid = SUBLANE - d, row8 < SUBLANE - d
                a_s = jnp.where(valid, pltpu.roll(a, shift, 0), 1.0)
                b_s = jnp.where(valid, pltpu.roll(b, shift, 0), 0.0)
                b = a * b_s + b
                a = a * a_s
            h = b + a * carry
            if direction == 0:
                h_ref[pl.ds(r0, SUBLANE), :] = h
                return h[SUBLANE - 1:SUBLANE, :]
            h_ref[pl.ds(r0, SUBLANE), :] += h
            return h[0:1, :]

        lax.fori_loop(0, nblk, block, jnp.zeros((1, width), F32), unroll=4)

    o_ref[0] = (h_ref[...] * jax.nn.gelu(y_ref[0])).astype(o_ref.dtype)


def _rglru(rg, conv_w, conv_b, wa, ba, wx, bx, lam, job):
    bsz, seq_len, w2 = rg.shape
    width = w2 // 2
    nh = RG_HEADS
    blk = width // nh
    wspec = pl.BlockSpec((2, 1, blk, blk), lambda b, h: (0, h, 0, 0))
    vspec = pl.BlockSpec((2, blk), lambda b, h: (0, h))
    return _call_with_cast(
        _rglru_kernel, job,
        (rg, rg, conv_w.astype(F32), conv_b.reshape(1, width).astype(F32),
         wa.astype(BF16), wx.astype(BF16), ba.astype(F32), bx.astype(F32), lam.astype(F32)),
        grid=(bsz, nh),
        in_specs=[pl.BlockSpec((1, seq_len, blk), lambda b, h: (b, 0, h)),
                  pl.BlockSpec((1, seq_len, blk), lambda b, h: (b, 0, nh + h)),
                  pl.BlockSpec((RG_CONV, blk), lambda b, h: (0, h)),
                  pl.BlockSpec((1, blk), lambda b, h: (0, h)),
                  wspec, wspec, vspec, vspec, vspec],
        out_specs=pl.BlockSpec((1, seq_len, blk), lambda b, h: (b, 0, h)),
        out_shape=jax.ShapeDtypeStruct((bsz, seq_len, width), BF16),
        scratch_shapes=[pltpu.VMEM((seq_len, blk), F32)] * 3,
        sem=("arbitrary", "arbitrary"),
        name="rglru",
    )


def _pad_cols(w, n):
    return jnp.pad(w, ((0, 0), (0, n - w.shape[1])))


def kernel(x, mix_norm, w_in, b_gate, hy_conv_w, hy_conv_b, hy_w1, hy_b1, hy_w2, hy_b2, hy_freq, hy_w3, hy_skip, ml_gate_b, ml_norm, rg_conv_w, rg_conv_b, rg_wa, rg_ba, rg_wx, rg_bx, rg_lambda, w_br_a, w_br_b, w_br_c, w_out, ffn_norm, w_gate, w_up, w_down, final_norm):
    bsz, seq_len, d_model = x.shape
    depth = mix_norm.shape[0]
    m = bsz * seq_len
    d_mix = d_model // 2
    dv = d_mix // ML_HEADS
    dk = dv // 2
    n_hy = (HY_ORDER + 1) * d_mix
    n_ml = 2 * ML_HEADS * dk + 2 * ML_HEADS * dv
    n_gt = 4 * ML_HEADS
    n_rg = 2 * d_mix
    o_ml = n_hy
    o_gt = o_ml + n_ml
    o_rg = o_gt + n_gt
    o_g = o_rg + n_rg
    ffn_hidden = w_gate.shape[2]
    ffn_pad = -(-ffn_hidden // FFN_PAD) * FFN_PAD

    half = seq_len // 2
    amat, bmat = _dft_tables(half)
    fidx = jnp.arange(half, dtype=F32) * (math.pi / seq_len)
    twc, tws = jnp.cos(fidx).reshape(half, 1), jnp.sin(fidx).reshape(half, 1)

    def deinterleave(a):
        return jnp.concatenate([a[0::2], a[1::2]], axis=0)

    t_pos, feat = _filter_features(seq_len)
    feat_pad = _pad_cols(deinterleave(feat), LANE)
    t_col = deinterleave(t_pos).reshape(seq_len, 1)
    deltas = jnp.abs(jnp.linspace(math.log(HY_DECAY_TARGET) / HY_SLOW_DECAY,
                                  math.log(HY_DECAY_TARGET) / HY_FAST_DECAY,
                                  d_mix, dtype=F32)).reshape(1, d_mix)

    w_in_t = jnp.swapaxes(w_in, 1, 2)
    h = x.reshape(m, d_model).astype(F32)
    for l in range(depth):
        u = _rmsnorm(h, mix_norm[l], BF16)
        w_hm = _cast_weight_t(w_in_t, l, 0, o_gt)
        u_split = u.reshape(bsz, half, 2, d_model).transpose(0, 2, 1, 3).reshape(m, d_model)
        hy = _matmul(u_split, w_hm, F32, "in_proj_hyena", 0, n_hy).reshape(bsz, seq_len, n_hy)
        ml = _matmul(u, w_hm, F32, "in_proj_mlstm", o_ml, n_ml).reshape(bsz, seq_len, n_ml)
        gt = _matmul(u, _cast_weight_t(w_in_t, l, o_gt, n_gt), F32, "in_proj_mlstm_gates")
        w_tail = _cast_weight_t(w_in_t, l, o_rg, n_rg + N_BRANCH * d_model)
        rg = _matmul(u, w_tail, F32, "in_proj_rglru", 0, n_rg)
        gate = _matmul_sigmoid(u, w_tail, b_gate[l].reshape(1, N_BRANCH * d_model).astype(F32),
                               "in_proj_merge_gates", n_rg)

        w1_pad = jnp.pad(hy_w1[l].astype(F32), ((0, LANE - HY_EMB), (0, 0)))
        hdn = _filter_mlp(feat_pad, w1_pad, hy_b1[l].astype(F32), hy_w2[l].astype(F32),
                          hy_b2[l].astype(F32), hy_freq[l].astype(F32))
        k_spec, k_mid = _filter_spectra(hdn, hy_w3[l].astype(F32), t_col, deltas, twc, tws, amat, bmat)
        y_a = _hyena(hy, hy_conv_w[l].astype(F32), hy_conv_b[l].reshape(1, n_hy).astype(F32),
                     hy_skip[l].astype(F32), k_spec, k_mid, twc, tws, amat, bmat)
        y_a = y_a.reshape(bsz, 2, half, d_mix).transpose(0, 2, 1, 3)

        mixer_steps = bsz * ML_HEADS
        y_b, w_g = _mlstm(ml, 0, gt.reshape(bsz, seq_len, n_gt), ml_gate_b[l], ml_norm[l],
                          _cast_job(w_gate, l, 1, 2 * LANE, ffn_pad, mixer_steps))
        y_c, w_u = _rglru(rg.reshape(bsz, seq_len, n_rg), rg_conv_w[l], rg_conv_b[l],
                          rg_wa[l], rg_ba[l], rg_wx[l], rg_bx[l], rg_lambda[l],
                          _cast_job(w_up, l, 1, 2 * LANE, ffn_pad, mixer_steps))

        merged = _merge(y_a.reshape(m, d_mix), y_b.reshape(m, d_mix), y_c.reshape(m, d_mix),
                        _cast_weight(w_br_a, l), _cast_weight(w_br_b, l), _cast_weight(w_br_c, l), gate)
        h = _matmul_residual_fullk(merged, _cast_weight(w_out, l), h, "out_proj")

        u = _rmsnorm(h, ffn_norm[l], BF16)
        gu_steps = (m // min(MM_TILE, m)) * (ffn_pad // (MM_TILE // 2))
        ff, w_d = _ffn_gate_up(u, w_g, w_u, _cast_job(w_down, l, 0, LANE, ffn_pad, gu_steps))
        h = _matmul_residual(ff, w_d, h, ffn_pad // 4, "ffn_down")

    out = _rmsnorm(h, final_norm, x.dtype)
    return out.reshape(bsz, seq_len, d_model)
```

```python
import functools
import math
from typing import NamedTuple

import jax
import jax.numpy as jnp
from jax import lax
from jax.experimental import pallas as pl
from jax.experimental.pallas import tpu as pltpu

F32 = jnp.float32
BF16 = jnp.bfloat16

N_BRANCH = 3
HY_ORDER = 2
HY_SHORT = 3
HY_EMB = 33
HY_BANDS = (HY_EMB - 1) // 2
HY_FILTER_HIDDEN = 64
HY_N_INNER = 2
HY_FAST_DECAY = 0.3
HY_SLOW_DECAY = 1.5
HY_DECAY_TARGET = 1e-2
ML_HEADS = 8
RG_HEADS = 8
RG_CONV = 4
RG_C = 8.0
EPS = 1e-6

V7X_VMEM_LIMIT_BYTES = 56 * 1024 * 1024
LANE = 128
SUBLANE = 8
MM_TILE = 1024
ML_CHUNK = 256
HY_CTILE = 256
HY_ROWS = 512
FFN_PAD = 1024
CAST_BLOCK_ELEMS = 1024 * 1024
DFT_SPLIT = 64


def _cparams(sem, vmem=V7X_VMEM_LIMIT_BYTES):
    return pltpu.CompilerParams(dimension_semantics=sem, vmem_limit_bytes=vmem)


def _dot(a, b):
    return jnp.dot(a, b, preferred_element_type=F32)


def _dot_hi(a, b):
    return jnp.dot(a, b, preferred_element_type=F32, precision=lax.Precision.HIGHEST)


def _cast_kernel(x_ref, o_ref, *, src_row_blocks, src_col_blocks):
    inside = (pl.program_id(0) < src_row_blocks) & (pl.program_id(1) < src_col_blocks)
    o_ref[...] = jnp.where(inside, x_ref[...], 0.0).astype(o_ref.dtype)


def _cast_weight(w_stack, layer, col0=0, ncols=None, out_rows=None, out_cols=None):
    _, k, n = w_stack.shape
    ncols = n - col0 if ncols is None else ncols
    out_rows = k if out_rows is None else out_rows
    out_cols = ncols if out_cols is None else out_cols
    tc = next(t for t in (4096, 2048, 1024, 512, 256, LANE)
              if ncols % t == 0 and out_cols % t == 0 and col0 % t == 0)
    tr = next(t for t in (2048, 1024, 512, 256)
              if k % t == 0 and out_rows % t == 0 and t * tc <= CAST_BLOCK_ELEMS)
    src_rb, src_cb, cb0 = k // tr, ncols // tc, col0 // tc
    kern = functools.partial(_cast_kernel, src_row_blocks=src_rb, src_col_blocks=src_cb)
    return pl.pallas_call(
        kern,
        grid=(out_rows // tr, out_cols // tc),
        in_specs=[pl.BlockSpec((None, tr, tc), lambda i, j: (layer, jnp.minimum(i, src_rb - 1),
                                                               cb0 + jnp.minimum(j, src_cb - 1)))],
        out_specs=pl.BlockSpec((tr, tc), lambda i, j: (i, j)),
        out_shape=jax.ShapeDtypeStruct((out_rows, out_cols), BF16),
        compiler_params=_cparams(("parallel", "parallel")),
        name="weight_cast",
    )(w_stack)


class _CastJob(NamedTuple):
    src: jax.Array
    layer: int
    axis: int
    block: tuple
    n_src: int
    n_out: int
    group: int

    @property
    def steps(self):
        return self.n_out // self.group

    @property
    def out_block(self):
        return tuple(d * self.group if a == self.axis else d for a, d in enumerate(self.block))

    @property
    def out_shape(self):
        return tuple(d * self.n_out if a == self.axis else d for a, d in enumerate(self.block))


def _cast_job(src, layer, axis, width, padded, host_steps):
    size = src.shape[1 + axis]
    assert size % width == 0 and padded % width == 0
    n_out = padded // width
    group = next(g for g in range(1, n_out + 1) if n_out % g == 0 and n_out // g <= host_steps)
    block = (width, src.shape[2]) if axis == 0 else (src.shape[1], width)
    return _CastJob(src, layer, axis, block, size // width, n_out, group)


def _linear_step(ids, grid):
    t = ids[0]
    for idx, n in zip(ids[1:], grid[1:]):
        t = t * n + idx
    return t


def _call_with_cast(body, job, args, *, grid, in_specs, out_specs, out_shape, scratch_shapes, sem, name):
    n_in = len(in_specs)
    assert job.steps <= math.prod(grid) and job.n_out % job.group == 0

    def local(ids):
        return jnp.minimum(_linear_step(ids, grid), job.steps - 1)

    def src_map(q):
        def index(*ids):
            blk = jnp.minimum(local(ids) * job.group + q, job.n_src - 1)
            return (job.layer, blk, 0) if job.axis == 0 else (job.layer, 0, blk)
        return index

    def dst_map(*ids):
        return (local(ids), 0) if job.axis == 0 else (0, local(ids))

    def kern(*refs):
        srcs = refs[n_in:n_in + job.group]
        host_out, dst = refs[n_in + job.group], refs[n_in + job.group + 1]
        body(*refs[:n_in], host_out, *refs[n_in + job.group + 2:])
        t = _linear_step([pl.program_id(a) for a in range(len(grid))], grid)

        @pl.when(t < job.steps)
        def _():
            parts = [jnp.where(t * job.group + q < job.n_src, src[...], 0.0) for q, src in enumerate(srcs)]
            dst[...] = jnp.concatenate(parts, axis=job.axis).astype(dst.dtype)

    src_specs = [pl.BlockSpec((None,) + tuple(job.block), src_map(q)) for q in range(job.group)]
    return pl.pallas_call(
        kern,
        grid=grid,
        in_specs=list(in_specs) + src_specs,
        out_specs=[out_specs, pl.BlockSpec(job.out_block, dst_map)],
        out_shape=[out_shape, jax.ShapeDtypeStruct(job.out_shape, BF16)],
        scratch_shapes=scratch_shapes,
        compiler_params=_cparams(sem),
        name=name,
    )(*args, *([job.src] * job.group))


def _cast_t_kernel(a_ref, o_ref):
    o_ref[...] = a_ref[...].T.astype(o_ref.dtype)


def _cast_t_shift_kernel(a_ref, b_ref, o_ref, *, shift):
    rows = jnp.concatenate([a_ref[shift:, :], b_ref[...]], axis=0)
    o_ref[...] = rows.T.astype(o_ref.dtype)


def _cast_weight_t(w_t, layer, row0, nrows):
    _, _, k = w_t.shape
    tr = min(CAST_BLOCK_ELEMS // k, nrows)
    shift = row0 % tr
    base = row0 - shift
    out_spec = pl.BlockSpec((k, tr), lambda i: (0, i))
    a_spec = pl.BlockSpec((None, tr, k), lambda i: (layer, base // tr + i, 0))
    common = dict(grid=(nrows // tr,), out_specs=out_spec,
                  out_shape=jax.ShapeDtypeStruct((k, nrows), BF16),
                  compiler_params=_cparams(("parallel",)))
    if shift == 0:
        return pl.pallas_call(_cast_t_kernel, in_specs=[a_spec], name="weight_cast_t", **common)(w_t)
    b_spec = pl.BlockSpec((None, shift, k), lambda i: (layer, (base + (i + 1) * tr) // shift, 0))
    kern = functools.partial(_cast_t_shift_kernel, shift=shift)
    return pl.pallas_call(kern, in_specs=[a_spec, b_spec], name="weight_cast_t_shifted", **common)(w_t, w_t)


def _rms_kernel(x_ref, g_ref, o_ref):
    x = x_ref[...]
    ms = jnp.mean(x * x, axis=-1, keepdims=True)
    o_ref[...] = (x * lax.rsqrt(ms + EPS) * g_ref[...]).astype(o_ref.dtype)


def _rmsnorm(x2d, g, out_dtype):
    m, d = x2d.shape
    tm = min(256, m)
    return pl.pallas_call(
        _rms_kernel,
        grid=(m // tm,),
        in_specs=[pl.BlockSpec((tm, d), lambda i: (i, 0)),
                  pl.BlockSpec((1, d), lambda i: (0, 0))],
        out_specs=pl.BlockSpec((tm, d), lambda i: (i, 0)),
        out_shape=jax.ShapeDtypeStruct((m, d), out_dtype),
        compiler_params=_cparams(("parallel",)),
        name="rmsnorm",
    )(x2d, g.reshape(1, d).astype(F32))


def _mm_kernel(a_ref, w_ref, o_ref):
    o_ref[...] = _dot(a_ref[...], w_ref[...]).astype(o_ref.dtype)


def _matmul(a, w, out_dtype, name, col0=0, ncols=None):
    m, k = a.shape
    n = w.shape[1] - col0 if ncols is None else ncols
    tm = min(MM_TILE, m)
    tn = min(MM_TILE, n)
    cb0 = col0 // tn
    return pl.pallas_call(
        _mm_kernel,
        grid=(m // tm, n // tn),
        in_specs=[pl.BlockSpec((tm, k), lambda i, j: (i, 0)),
                  pl.BlockSpec((k, tn), lambda i, j: (0, cb0 + j))],
        out_specs=pl.BlockSpec((tm, tn), lambda i, j: (i, j)),
        out_shape=jax.ShapeDtypeStruct((m, n), out_dtype),
        compiler_params=_cparams(("parallel", "parallel")),
        name=name,
    )(a, w)


def _mm_sigmoid_kernel(a_ref, w_ref, b_ref, o_ref):
    o_ref[...] = jax.nn.sigmoid(_dot(a_ref[...], w_ref[...]) + b_ref[...]).astype(o_ref.dtype)


def _matmul_sigmoid(a, w, bias, name, col0=0):
    m, k = a.shape
    n = w.shape[1] - col0
    tm = min(MM_TILE, m)
    tn = min(MM_TILE, n)
    cb0 = col0 // tn
    return pl.pallas_call(
        _mm_sigmoid_kernel,
        grid=(m // tm, n // tn),
        in_specs=[pl.BlockSpec((tm, k), lambda i, j: (i, 0)),
                  pl.BlockSpec((k, tn), lambda i, j: (0, cb0 + j)),
                  pl.BlockSpec((1, tn), lambda i, j: (0, j))],
        out_specs=pl.BlockSpec((tm, tn), lambda i, j: (i, j)),
        out_shape=jax.ShapeDtypeStruct((m, n), BF16),
        compiler_params=_cparams(("parallel", "parallel")),
        name=name,
    )(a, w, bias)


def _mm_res_kernel(a_ref, w_ref, r_ref, o_ref):
    k = pl.program_id(2)

    @pl.when(k == 0)
    def _():
        o_ref[...] = r_ref[...]

    o_ref[...] += _dot(a_ref[...], w_ref[...])


def _matmul_residual(a, w, res, tk, name):
    m, k = a.shape
    n = w.shape[1]
    tm = min(MM_TILE, m)
    tn = min(MM_TILE, n)
    return pl.pallas_call(
        _mm_res_kernel,
        grid=(m // tm, n // tn, k // tk),
        in_specs=[pl.BlockSpec((tm, tk), lambda i, j, kk: (i, kk)),
                  pl.BlockSpec((tk, tn), lambda i, j, kk: (kk, j)),
                  pl.BlockSpec((tm, tn), lambda i, j, kk: (i, j))],
        out_specs=pl.BlockSpec((tm, tn), lambda i, j, kk: (i, j)),
        out_shape=jax.ShapeDtypeStruct((m, n), F32),
        compiler_params=_cparams(("parallel", "parallel", "arbitrary")),
        name=name,
    )(a, w, res)


def _mm_res_fullk_kernel(a_ref, w_ref, r_ref, o_ref):
    o_ref[...] = r_ref[...] + _dot(a_ref[...], w_ref[...])


def _matmul_residual_fullk(a, w, res, name):
    m, k = a.shape
    n = w.shape[1]
    tm = min(MM_TILE, m)
    tn = min(MM_TILE // 2, n)
    return pl.pallas_call(
        _mm_res_fullk_kernel,
        grid=(m // tm, n // tn),
        in_specs=[pl.BlockSpec((tm, k), lambda i, j: (i, 0)),
                  pl.BlockSpec((k, tn), lambda i, j: (0, j)),
                  pl.BlockSpec((tm, tn), lambda i, j: (i, j))],
        out_specs=pl.BlockSpec((tm, tn), lambda i, j: (i, j)),
        out_shape=jax.ShapeDtypeStruct((m, n), F32),
        compiler_params=_cparams(("parallel", "parallel")),
        name=name,
    )(a, w, res)


def _ffn_gu_kernel(a_ref, wg_ref, wu_ref, o_ref):
    a = a_ref[...]
    g = _dot(a, wg_ref[...])
    u = _dot(a, wu_ref[...])
    o_ref[...] = (g * jax.nn.sigmoid(g) * u).astype(o_ref.dtype)


def _ffn_gate_up(a, wg, wu, job):
    m, k = a.shape
    n = wg.shape[1]
    tm = min(MM_TILE, m)
    tn = MM_TILE // 2
    return _call_with_cast(
        _ffn_gu_kernel, job, (a, wg, wu),
        grid=(m // tm, n // tn),
        in_specs=[pl.BlockSpec((tm, k), lambda i, j: (i, 0)),
                  pl.BlockSpec((k, tn), lambda i, j: (0, j)),
                  pl.BlockSpec((k, tn), lambda i, j: (0, j))],
        out_specs=pl.BlockSpec((tm, tn), lambda i, j: (i, j)),
        out_shape=jax.ShapeDtypeStruct((m, n), BF16),
        scratch_shapes=[],
        sem=("arbitrary", "arbitrary"),
        name="ffn_gate_up",
    )


def _merge_kernel(ya_ref, yb_ref, yc_ref, wa_ref, wb_ref, wc_ref, ga_ref, gb_ref, gc_ref, o_ref):
    acc = ga_ref[...].astype(F32) * _dot(ya_ref[...], wa_ref[...])
    acc += gb_ref[...].astype(F32) * _dot(yb_ref[...], wb_ref[...])
    acc += gc_ref[...].astype(F32) * _dot(yc_ref[...], wc_ref[...])
    o_ref[...] = acc.astype(o_ref.dtype)


def _merge(ya, yb, yc, wa, wb, wc, gate):
    m, c = ya.shape
    d = wa.shape[1]
    tm = min(MM_TILE, m)
    tn = MM_TILE // 2
    nb = d // tn
    yspec = pl.BlockSpec((tm, c), lambda i, j: (i, 0))
    wspec = pl.BlockSpec((c, tn), lambda i, j: (0, j))

    def gspec(br):
        return pl.BlockSpec((tm, tn), lambda i, j: (i, br * nb + j))

    return pl.pallas_call(
        _merge_kernel,
        grid=(m // tm, nb),
        in_specs=[yspec, yspec, yspec, wspec, wspec, wspec, gspec(0), gspec(1), gspec(2)],
        out_specs=pl.BlockSpec((tm, tn), lambda i, j: (i, j)),
        out_shape=jax.ShapeDtypeStruct((m, d), BF16),
        compiler_params=_cparams(("parallel", "parallel")),
        name="branch_merge",
    )(ya, yb, yc, wa, wb, wc, gate, gate, gate)


def _dft_tables(seq_len):
    f = jnp.arange(seq_len, dtype=jnp.int32)

    def trig(s):
        ang = ((f[:, None] * s[None, :]) % (2 * seq_len)).astype(F32) * (math.pi / seq_len)
        return jnp.cos(ang), jnp.sin(ang)

    c0, s0 = trig(jnp.arange(DFT_SPLIT, dtype=jnp.int32))
    c1, s1 = trig(jnp.arange(seq_len // DFT_SPLIT, dtype=jnp.int32) * DFT_SPLIT)
    cm = c1[:, :, None] * c0[:, None, :] - s1[:, :, None] * s0[:, None, :]
    sm = s1[:, :, None] * c0[:, None, :] + c1[:, :, None] * s0[:, None, :]
    return cm.reshape(seq_len, seq_len).astype(BF16), sm.reshape(seq_len, seq_len).astype(BF16)


def _filter_features(seq_len):
    pos = jnp.arange(seq_len, dtype=F32)
    t = pos / (seq_len - 1)
    omega = 2.0 * math.pi * pos / seq_len
    bands = jnp.linspace(1e-4, HY_BANDS - 1, HY_BANDS, dtype=F32)
    ang = omega[:, None] * bands[None, :]
    feat = jnp.concatenate([t[:, None], jnp.cos(ang), -jnp.sin(ang)], axis=-1)
    return t, feat


def _filter_mlp_kernel(feat_ref, w1_ref, b1_ref, w2_ref, b2_ref, freq_ref, o_ref):
    freq = freq_ref[...]
    hdn = jnp.sin(freq * (_dot_hi(feat_ref[...], w1_ref[...]) + b1_ref[...]))
    for j in range(HY_N_INNER):
        hdn = jnp.sin(freq * (_dot_hi(hdn, w2_ref[j]) + b2_ref[j]))
    o_ref[...] = hdn


def _filter_mlp(feat_pad, w1_pad, b1, w2, b2, freq):
    seq_len = feat_pad.shape[0]
    hid = w1_pad.shape[1]
    return pl.pallas_call(
        _filter_mlp_kernel,
        out_shape=jax.ShapeDtypeStruct((seq_len, hid), F32),
        name="hyena_filter_mlp",
    )(feat_pad, w1_pad, b1.reshape(1, hid), w2, b2.reshape(HY_N_INNER, 1, hid), freq.reshape(1, hid))


def _half_spectrum(a_ref, b_ref, twc, tws, even, odd, rows):
    ae, be = _dot(a_ref[rows, :], even), _dot(b_ref[rows, :], even)
    ao, bo = _dot(a_ref[rows, :], odd), _dot(b_ref[rows, :], odd)
    oc = twc * ao - tws * bo
    os_ = twc * bo + tws * ao
    return ae + oc, ae - oc, be + os_, os_ - be


def _filter_spec_kernel(hdn_ref, wf_ref, wb_ref, t_ref, dl_ref, twc_ref, tws_ref, a_ref, b_ref,
                        k_ref, kmid_ref):
    seq_len = hdn_ref.shape[0]
    half = seq_len // 2
    hdn = hdn_ref[...]
    win = jnp.exp(-t_ref[...] * dl_ref[...])
    row = lax.broadcasted_iota(jnp.int32, (seq_len, 1), 0)
    fwd = _dot_hi(hdn, wf_ref[...]) * win
    bwd = jnp.where(row == 0, 0.0, _dot_hi(hdn, wb_ref[...]) * win)
    nrm = lax.rsqrt(jnp.sum(fwd * fwd, axis=0, keepdims=True)
                    + jnp.sum(bwd * bwd, axis=0, keepdims=True))
    sym = (fwd + bwd) * nrm
    asym = (bwd - fwd) * nrm
    hrow = lax.broadcasted_iota(jnp.int32, (half, 1), 0)
    alt = (1 - 2 * (hrow & 1)).astype(F32)
    wfreq = jnp.where(hrow == 0, 0.5 / seq_len, 1.0 / seq_len)
    rows = slice(0, half)
    twc, tws = twc_ref[...], tws_ref[...]
    kre_lo, kre_hi, _, _ = _half_spectrum(a_ref, b_ref, twc, tws, sym[:half].astype(BF16),
                                          sym[half:].astype(BF16), rows)
    _, _, kim_lo, kim_hi = _half_spectrum(a_ref, b_ref, twc, tws, asym[:half].astype(BF16),
                                          asym[half:].astype(BF16), rows)
    k_ref[0, 0] = (kre_lo * wfreq).astype(k_ref.dtype)
    k_ref[0, 1] = (kim_lo * wfreq).astype(k_ref.dtype)
    k_ref[0, 2] = (kre_hi * wfreq).astype(k_ref.dtype)
    k_ref[0, 3] = (kim_hi * wfreq).astype(k_ref.dtype)
    kmid_ref[0, 0:1, :] = jnp.sum(alt * sym[:half], axis=0, keepdims=True) * (1.0 / seq_len)
    kmid_ref[0, 1:2, :] = jnp.sum(alt * asym[half:], axis=0, keepdims=True) * (1.0 / seq_len)


def _filter_spectra(hdn, w3, t_col, deltas, twc, tws, amat, bmat):
    seq_len, hid = hdn.shape
    half = seq_len // 2
    c = deltas.shape[1]
    tc = HY_CTILE
    nct = c // tc
    const2 = lambda o, j: (0, 0)
    return pl.pallas_call(
        _filter_spec_kernel,
        grid=(HY_ORDER, nct),
        in_specs=[pl.BlockSpec((seq_len, hid), const2),
                  pl.BlockSpec((hid, tc), lambda o, j: (0, o * 2 * nct + j)),
                  pl.BlockSpec((hid, tc), lambda o, j: (0, o * 2 * nct + nct + j)),
                  pl.BlockSpec((seq_len, 1), const2),
                  pl.BlockSpec((1, tc), lambda o, j: (0, j)),
                  pl.BlockSpec((half, 1), const2),
                  pl.BlockSpec((half, 1), const2),
                  pl.BlockSpec((half, half), const2, pipeline_mode=pl.Buffered(1)),
                  pl.BlockSpec((half, half), const2, pipeline_mode=pl.Buffered(1))],
        out_specs=[pl.BlockSpec((1, 4, half, tc), lambda o, j: (o, 0, 0, j)),
                   pl.BlockSpec((1, 2, tc), lambda o, j: (o, 0, j))],
        out_shape=[jax.ShapeDtypeStruct((HY_ORDER, 4, half, c), BF16),
                   jax.ShapeDtypeStruct((HY_ORDER, 2, c), F32)],
        compiler_params=_cparams(("parallel", "parallel")),
        name="hyena_filter_spectra",
    )(hdn, w3, w3, t_col, deltas, twc, tws, amat, bmat)


def _short_conv(xe, xo, w, b, half):
    row = lax.broadcasted_iota(jnp.int32, (half, 1), 0)
    o_prev = jnp.where(row == 0, 0.0, pltpu.roll(xo, 1, 0))
    e_next = jnp.where(row == half - 1, 0.0, pltpu.roll(xe, half - 1, 0))
    ye = o_prev * w[0:1] + xe * w[1:2] + xo * w[2:3] + b
    yo = xe * w[0:1] + xo * w[1:2] + e_next * w[2:3] + b
    return ye, yo


def _hyena_kernel(x1_ref, x2_ref, v_ref, w1_ref, w2_ref, wv_ref, b1_ref, b2_ref, bv_ref,
                  skip_ref, k_ref, kmid_ref, twc_ref, tws_ref, a_ref, b_ref, o_ref,
                  sig_ref, gate_ref, ue_ref, ve_ref, uo_ref, vo_ref):
    seq_len = v_ref.shape[1]
    half = seq_len // 2
    chunk = min(HY_ROWS, half)
    nchunk = half // chunk
    hrow = lax.broadcasted_iota(jnp.int32, (half, 1), 0)
    alt = (1 - 2 * (hrow & 1)).astype(F32)

    def long_conv(order, epilogue):
        sig_e, sig_o = sig_ref[0:half, :], sig_ref[half:seq_len, :]
        e_b, o_b = sig_e.astype(BF16), sig_o.astype(BF16)
        r_mid = jnp.sum(alt * sig_e, axis=0, keepdims=True)
        p_mid = jnp.sum(alt * sig_o, axis=0, keepdims=True)
        kre_mid, kim_mid = kmid_ref[order, 0:1, :], kmid_ref[order, 1:2, :]
        u_mid = r_mid * kre_mid + p_mid * kim_mid
        v_mid = p_mid * kre_mid - r_mid * kim_mid
        for ch in range(nchunk):
            rows = slice(ch * chunk, (ch + 1) * chunk)
            twc, tws = twc_ref[rows, :], tws_ref[rows, :]
            r_lo, r_hi, p_lo, p_hi = _half_spectrum(a_ref, b_ref, twc, tws, e_b, o_b, rows)
            kre_lo = k_ref[order, 0, rows, :].astype(F32)
            kim_lo = k_ref[order, 1, rows, :].astype(F32)
            kre_hi = k_ref[order, 2, rows, :].astype(F32)
            kim_hi = k_ref[order, 3, rows, :].astype(F32)
            u_lo = r_lo * kre_lo + p_lo * kim_lo
            v_lo = p_lo * kre_lo - r_lo * kim_lo
            u_hi = r_hi * kre_hi + p_hi * kim_hi
            v_hi = p_hi * kre_hi - r_hi * kim_hi
            ue_ref[rows, :] = (u_lo + u_hi).astype(BF16)
            ve_ref[rows, :] = (v_lo - v_hi).astype(BF16)
            u_od, v_od = u_lo - u_hi, v_lo + v_hi
            uo_ref[rows, :] = (twc * u_od + tws * v_od).astype(BF16)
            vo_ref[rows, :] = (twc * v_od - tws * u_od).astype(BF16)
        for ch in range(nchunk):
            rows = slice(ch * chunk, (ch + 1) * chunk)
            y_e = _dot(a_ref[rows, :], ue_ref[...]) + _dot(b_ref[rows, :], ve_ref[...]) + alt[rows] * u_mid
            y_o = _dot(a_ref[rows, :], uo_ref[...]) + _dot(b_ref[rows, :], vo_ref[...]) + alt[rows] * v_mid
            epilogue(rows, y_e, y_o)

    def conv3(x_ref, w_ref, b_ref_, dst_ref):
        x = x_ref[0]
        ye, yo = _short_conv(x[:half], x[half:], w_ref[...], b_ref_[...], half)
        dst_ref[0:half, :] = ye
        dst_ref[half:seq_len, :] = yo

    conv3(v_ref, wv_ref, bv_ref, sig_ref)
    conv3(x1_ref, w1_ref, b1_ref, gate_ref)

    def first(rows, y_e, y_o):
        for off, y in ((0, y_e), (half, y_o)):
            r = slice(off + rows.start, off + rows.stop)
            sig_ref[r, :] = gate_ref[r, :] * (y + skip_ref[0:1, :] * sig_ref[r, :])

    long_conv(0, first)
    conv3(x2_ref, w2_ref, b2_ref, gate_ref)

    def second(rows, y_e, y_o):
        for off, y in ((0, y_e), (half, y_o)):
            r = slice(off + rows.start, off + rows.stop)
            o_ref[0, r, :] = (gate_ref[r, :] * (y + skip_ref[1:2, :] * sig_ref[r, :])).astype(o_ref.dtype)

    long_conv(1, second)


def _hyena(hy, conv_w, conv_b, skip, k, kmid, twc, tws, amat, bmat):
    bsz, seq_len, _ = hy.shape
    half = seq_len // 2
    c = skip.shape[1]
    tc = HY_CTILE
    nct = c // tc

    def xspec(part):
        return pl.BlockSpec((1, seq_len, tc), lambda j, b: (b, 0, part * nct + j))

    def wspec(part):
        return pl.BlockSpec((HY_SHORT, tc), lambda j, b: (0, part * nct + j))

    def bspec(part):
        return pl.BlockSpec((1, tc), lambda j, b: (0, part * nct + j))

    const2 = lambda j, b: (0, 0)
    return pl.pallas_call(
        _hyena_kernel,
        grid=(nct, bsz),
        in_specs=[xspec(0), xspec(1), xspec(2), wspec(0), wspec(1), wspec(2),
                  bspec(0), bspec(1), bspec(2),
                  pl.BlockSpec((HY_ORDER, tc), lambda j, b: (0, j)),
                  pl.BlockSpec((HY_ORDER, 4, half, tc), lambda j, b: (0, 0, 0, j)),
                  pl.BlockSpec((HY_ORDER, 2, tc), lambda j, b: (0, 0, j)),
                  pl.BlockSpec((half, 1), const2),
                  pl.BlockSpec((half, 1), const2),
                  pl.BlockSpec((half, half), const2, pipeline_mode=pl.Buffered(1)),
                  pl.BlockSpec((half, half), const2, pipeline_mode=pl.Buffered(1))],
        out_specs=pl.BlockSpec((1, seq_len, tc), lambda j, b: (b, 0, j)),
        out_shape=jax.ShapeDtypeStruct((bsz, seq_len, c), BF16),
        scratch_shapes=[pltpu.VMEM((seq_len, tc), F32), pltpu.VMEM((seq_len, tc), F32)]
        + [pltpu.VMEM((half, tc), BF16)] * 4,
        compiler_params=_cparams(("parallel", "arbitrary")),
        name="hyena_conv",
    )(hy, hy, hy, conv_w, conv_w, conv_w, conv_b, conv_b, conv_b, skip, k, kmid, twc, tws, amat, bmat)


def _parity_selectors(tile, transposed):
    r = jnp.arange(tile // 2, dtype=jnp.int32)[:, None]
    c = jnp.arange(tile, dtype=jnp.int32)[None, :]
    sel = jnp.stack([c == 2 * r, c == 2 * r + 1]).astype(BF16)
    return jnp.swapaxes(sel, 1, 2) if transposed else sel


def _split_kernel(x_ref, sel_ref, o_ref):
    x = x_ref[...]
    o_ref[0, 0] = _dot(sel_ref[0], x).astype(o_ref.dtype)
    o_ref[0, 1] = _dot(sel_ref[1], x).astype(o_ref.dtype)


def _split_even_odd(x, bsz, seq_len):
    _, d = x.shape
    tile = min(HY_ROWS, seq_len)
    nt = seq_len // tile
    return pl.pallas_call(
        _split_kernel,
        grid=(bsz, nt),
        in_specs=[pl.BlockSpec((tile, d), lambda b, i: (b * nt + i, 0)),
                  pl.BlockSpec((2, tile // 2, tile), lambda b, i: (0, 0, 0))],
        out_specs=pl.BlockSpec((1, 2, tile // 2, d), lambda b, i: (b, 0, i, 0)),
        out_shape=jax.ShapeDtypeStruct((bsz, 2, seq_len // 2, d), x.dtype),
        compiler_params=_cparams(("parallel", "parallel")),
        name="split_even_odd",
    )(x, _parity_selectors(tile, False))


def _interleave_kernel(y_ref, sel_ref, o_ref):
    o_ref[...] = (_dot(sel_ref[0], y_ref[0, 0]) + _dot(sel_ref[1], y_ref[0, 1])).astype(o_ref.dtype)


def _interleave_even_odd(y):
    bsz, _, half, c = y.shape
    seq_len = 2 * half
    tile = min(HY_ROWS, seq_len)
    nt = seq_len // tile
    return pl.pallas_call(
        _interleave_kernel,
        grid=(bsz, nt),
        in_specs=[pl.BlockSpec((1, 2, tile // 2, c), lambda b, i: (b, 0, i, 0)),
                  pl.BlockSpec((2, tile, tile // 2), lambda b, i: (0, 0, 0))],
        out_specs=pl.BlockSpec((tile, c), lambda b, i: (b * nt + i, 0)),
        out_shape=jax.ShapeDtypeStruct((bsz * seq_len, c), y.dtype),
        compiler_params=_cparams(("parallel", "parallel")),
        name="interleave_even_odd",
    )(y, _parity_selectors(tile, True))


def _mlstm_kernel(q_ref, k_ref, v_ref, og_ref, kt_ref, gc_ref, gr_ref, bc_ref, br_ref, nw_ref,
                  o_ref, hf_ref, hb_ref, *, chunk, nchunks, dk, dv):
    rowi = lax.broadcasted_iota(jnp.int32, (chunk, chunk), 0)
    coli = lax.broadcasted_iota(jnp.int32, (chunk, chunk), 1)
    lower = coli <= rowi
    upper = coli >= rowi
    bias_c = bc_ref[0]
    bias_r = br_ref[0]
    scale = dk ** -0.5
    neg_inf = -jnp.inf

    def step(c, direction, state):
        c_st, n_st, m_st = state
        r0 = pl.multiple_of(c * chunk, chunk)
        gcol = gc_ref[0, 0, pl.ds(r0, chunk), :] + bias_c
        grow = gr_ref[0, 0, c] + bias_r
        gi, gf = 2 * direction, 2 * direction + 1
        i_col = gcol[:, gi:gi + 1]
        f_col = jax.nn.log_sigmoid(gcol[:, gf:gf + 1])
        i_row = grow[gi:gi + 1, :]
        f_row = jax.nn.log_sigmoid(grow[gf:gf + 1, :])
        mask, mask_t = (lower, upper) if direction == 0 else (upper, lower)
        b_col = jnp.sum(jnp.where(mask, f_row, 0.0), axis=1, keepdims=True)
        b_row = jnp.sum(jnp.where(mask_t, f_col, 0.0), axis=0, keepdims=True)
        g = jnp.sum(f_row, axis=1, keepdims=True)
        dmat = jnp.where(mask, b_col - b_row + i_row, neg_inf)
        inter = b_col + m_st
        m_j = jnp.maximum(inter, jnp.max(dmat, axis=1, keepdims=True))
        w_intra = jnp.exp(dmat - m_j)
        w_inter = jnp.exp(inter - m_j)
        qc = q_ref[0, pl.ds(r0, chunk), :] * scale
        qb = qc.astype(BF16)
        kt = kt_ref[0, 0, c].astype(BF16)
        kc = k_ref[0, pl.ds(r0, chunk), :]
        vc = v_ref[0, pl.ds(r0, chunk), :]
        s = _dot(qb, kt) * w_intra
        num = w_inter * _dot(qb, c_st.astype(BF16)) + _dot(s.astype(BF16), vc.astype(BF16))
        den = (w_inter * jnp.sum(qc * n_st, axis=1, keepdims=True)
               + jnp.sum(s, axis=1, keepdims=True))
        h = num / jnp.maximum(jnp.abs(den), jnp.exp(-m_j))
        lw = g - b_col + i_col
        m_new = jnp.maximum(g + m_st, jnp.max(lw, axis=0, keepdims=True))
        wl = jnp.exp(lw - m_new)
        decay = jnp.exp(g + m_st - m_new)
        c_new = decay * c_st + _dot(kt, (wl * vc).astype(BF16))
        n_new = decay * n_st + jnp.sum(wl * kc, axis=0, keepdims=True)
        return h, (c_new, n_new, m_new)

    def body(c, carry):
        st_f, st_b = carry
        h_f, st_f = step(c, 0, st_f)
        hf_ref[pl.ds(pl.multiple_of(c * chunk, chunk), chunk), :] = h_f
        cb = nchunks - 1 - c
        h_b, st_b = step(cb, 1, st_b)
        hb_ref[pl.ds(pl.multiple_of(cb * chunk, chunk), chunk), :] = h_b
        return st_f, st_b

    zero = (jnp.zeros((dk, dv), F32), jnp.zeros((1, dk), F32), jnp.zeros((1, 1), F32))
    lax.fori_loop(0, nchunks, body, (zero, zero))

    hsum = hf_ref[...] + hb_ref[...]
    hn = hsum * lax.rsqrt(jnp.mean(hsum * hsum, axis=-1, keepdims=True) + EPS) * nw_ref[...]
    o_ref[0] = (jax.nn.sigmoid(og_ref[0]) * hn).astype(o_ref.dtype)


def _mlstm(ml, col0, gates, gate_b, norm_w, job):
    bsz, seq_len, _ = ml.shape
    nh = ML_HEADS
    dv = norm_w.shape[0] // nh
    dk = dv // 2
    chunk = min(ML_CHUNK, seq_len)
    nchunks = seq_len // chunk
    koff = col0 + nh * dk
    qb0, vb0 = col0 // dk, (col0 + 2 * nh * dk) // dv
    k_t = (ml[:, :, koff:koff + nh * dk].reshape(bsz, nchunks, chunk, nh, dk)
           .transpose(0, 3, 1, 4, 2))
    g4 = gates.reshape(bsz, seq_len, 4, nh)
    g_col = g4.transpose(0, 3, 1, 2)
    g_row = g4.reshape(bsz, nchunks, chunk, 4, nh).transpose(0, 4, 1, 3, 2)
    gb = gate_b.astype(F32).reshape(4, nh)
    b_col = gb.T.reshape(nh, 1, 4)
    b_row = gb.T.reshape(nh, 4, 1)
    kern = functools.partial(_mlstm_kernel, chunk=chunk, nchunks=nchunks, dk=dk, dv=dv)
    return _call_with_cast(
        kern, job,
        (ml, ml, ml, ml, k_t, g_col, g_row, b_col, b_row, norm_w.reshape(1, nh * dv).astype(F32)),
        grid=(bsz, nh),
        in_specs=[pl.BlockSpec((1, seq_len, dk), lambda b, h: (b, 0, qb0 + h)),
                  pl.BlockSpec((1, seq_len, dk), lambda b, h: (b, 0, qb0 + nh + h)),
                  pl.BlockSpec((1, seq_len, dv), lambda b, h: (b, 0, vb0 + h)),
                  pl.BlockSpec((1, seq_len, dv), lambda b, h: (b, 0, vb0 + nh + h)),
                  pl.BlockSpec((1, 1, nchunks, dk, chunk), lambda b, h: (b, h, 0, 0, 0)),
                  pl.BlockSpec((1, 1, seq_len, 4), lambda b, h: (b, h, 0, 0)),
                  pl.BlockSpec((1, 1, nchunks, 4, chunk), lambda b, h: (b, h, 0, 0, 0)),
                  pl.BlockSpec((1, 1, 4), lambda b, h: (h, 0, 0)),
                  pl.BlockSpec((1, 4, 1), lambda b, h: (h, 0, 0)),
                  pl.BlockSpec((1, dv), lambda b, h: (0, h))],
        out_specs=pl.BlockSpec((1, seq_len, dv), lambda b, h: (b, 0, h)),
        out_shape=jax.ShapeDtypeStruct((bsz, seq_len, nh * dv), BF16),
        scratch_shapes=[pltpu.VMEM((seq_len, dv), F32), pltpu.VMEM((seq_len, dv), F32)],
        sem=("arbitrary", "arbitrary"),
        name="mlstm",
    )


def _rglru_kernel(x_ref, y_ref, cw_ref, cb_ref, wa_ref, wx_ref, ba_ref, bx_ref, lam_ref,
                  o_ref, a_ref, b_ref, h_ref):
    seq_len, width = x_ref.shape[1], x_ref.shape[2]
    nblk = seq_len // SUBLANE
    x = x_ref[0]
    row = lax.broadcasted_iota(jnp.int32, (seq_len, 1), 0)
    cw = cw_ref[...]
    xc = (jnp.where(row < 2, 0.0, pltpu.roll(x, 2, 0)) * cw[0:1]
          + jnp.where(row < 1, 0.0, pltpu.roll(x, 1, 0)) * cw[1:2]
          + x * cw[2:3]
          + jnp.where(row == seq_len - 1, 0.0, pltpu.roll(x, seq_len - 1, 0)) * cw[3:4]
          + cb_ref[...])
    xb = xc.astype(BF16)
    row8 = lax.broadcasted_iota(jnp.int32, (SUBLANE, width), 0)

    for direction in range(2):
        r = jax.nn.sigmoid(_dot(xb, wa_ref[direction, 0]) + ba_ref[direction:direction + 1, :])
        ig = jax.nn.sigmoid(_dot(xb, wx_ref[direction, 0]) + bx_ref[direction:direction + 1, :])
        log_a = -RG_C * r * jax.nn.softplus(-lam_ref[direction:direction + 1, :])
        a_ref[...] = jnp.exp(log_a)
        th = jnp.tanh(log_a)
        b_ref[...] = jnp.sqrt(-2.0 * th / (1.0 - th)) * (ig * xc)

        def block(i, carry, direction=direction):
            blk = i if direction == 0 else nblk - 1 - i
            r0 = pl.multiple_of(blk * SUBLANE, SUBLANE)
            a = a_ref[pl.ds(r0, SUBLANE), :]
            b = b_ref[pl.ds(r0, SUBLANE), :]
            for d in (1, 2, 4):
                if direction == 0:
                    shift, valid = d, row8 >= d
                else:
                    shift, valid = SUBLANE - d, row8 < SUBLANE - d
                a_s = jnp.where(valid, pltpu.roll(a, shift, 0), 1.0)
                b_s = jnp.where(valid, pltpu.roll(b, shift, 0), 0.0)
                b = a * b_s + b
                a = a * a_s
            h = b + a * carry
            if direction == 0:
                h_ref[pl.ds(r0, SUBLANE), :] = h
                return h[SUBLANE - 1:SUBLANE, :]
            h_ref[pl.ds(r0, SUBLANE), :] += h
            return h[0:1, :]

        lax.fori_loop(0, nblk, block, jnp.zeros((1, width), F32), unroll=4)

    o_ref[0] = (h_ref[...] * jax.nn.gelu(y_ref[0])).astype(o_ref.dtype)


def _rglru(rg, conv_w, conv_b, wa, ba, wx, bx, lam, job):
    bsz, seq_len, w2 = rg.shape
    width = w2 // 2
    nh = RG_HEADS
    blk = width // nh
    wspec = pl.BlockSpec((2, 1, blk, blk), lambda b, h: (0, h, 0, 0))
    vspec = pl.BlockSpec((2, blk), lambda b, h: (0, h))
    return _call_with_cast(
        _rglru_kernel, job,
        (rg, rg, conv_w.astype(F32), conv_b.reshape(1, width).astype(F32),
         wa.astype(BF16), wx.astype(BF16), ba.astype(F32), bx.astype(F32), lam.astype(F32)),
        grid=(bsz, nh),
        in_specs=[pl.BlockSpec((1, seq_len, blk), lambda b, h: (b, 0, h)),
                  pl.BlockSpec((1, seq_len, blk), lambda b, h: (b, 0, nh + h)),
                  pl.BlockSpec((RG_CONV, blk), lambda b, h: (0, h)),
                  pl.BlockSpec((1, blk), lambda b, h: (0, h)),
                  wspec, wspec, vspec, vspec, vspec],
        out_specs=pl.BlockSpec((1, seq_len, blk), lambda b, h: (b, 0, h)),
        out_shape=jax.ShapeDtypeStruct((bsz, seq_len, width), BF16),
        scratch_shapes=[pltpu.VMEM((seq_len, blk), F32)] * 3,
        sem=("arbitrary", "arbitrary"),
        name="rglru",
    )


def _pad_cols(w, n):
    return jnp.pad(w, ((0, 0), (0, n - w.shape[1])))


def kernel(x, mix_norm, w_in, b_gate, hy_conv_w, hy_conv_b, hy_w1, hy_b1, hy_w2, hy_b2, hy_freq, hy_w3, hy_skip, ml_gate_b, ml_norm, rg_conv_w, rg_conv_b, rg_wa, rg_ba, rg_wx, rg_bx, rg_lambda, w_br_a, w_br_b, w_br_c, w_out, ffn_norm, w_gate, w_up, w_down, final_norm):
    bsz, seq_len, d_model = x.shape
    depth = mix_norm.shape[0]
    m = bsz * seq_len
    d_mix = d_model // 2
    dv = d_mix // ML_HEADS
    dk = dv // 2
    n_hy = (HY_ORDER + 1) * d_mix
    n_ml = 2 * ML_HEADS * dk + 2 * ML_HEADS * dv
    n_gt = 4 * ML_HEADS
    n_rg = 2 * d_mix
    o_ml = n_hy
    o_gt = o_ml + n_ml
    o_rg = o_gt + n_gt
    o_g = o_rg + n_rg
    ffn_hidden = w_gate.shape[2]
    ffn_pad = -(-ffn_hidden // FFN_PAD) * FFN_PAD

    half = seq_len // 2
    amat, bmat = _dft_tables(half)
    fidx = jnp.arange(half, dtype=F32) * (math.pi / seq_len)
    twc, tws = jnp.cos(fidx).reshape(half, 1), jnp.sin(fidx).reshape(half, 1)

    def deinterleave(a):
        return jnp.concatenate([a[0::2], a[1::2]], axis=0)

    t_pos, feat = _filter_features(seq_len)
    feat_pad = _pad_cols(deinterleave(feat), LANE)
    t_col = deinterleave(t_pos).reshape(seq_len, 1)
    deltas = jnp.abs(jnp.linspace(math.log(HY_DECAY_TARGET) / HY_SLOW_DECAY,
                                  math.log(HY_DECAY_TARGET) / HY_FAST_DECAY,
                                  d_mix, dtype=F32)).reshape(1, d_mix)

    w_in_t = jnp.swapaxes(w_in, 1, 2)
    h = x.reshape(m, d_model).astype(F32)
    for l in range(depth):
        u = _rmsnorm(h, mix_norm[l], BF16)
        w_hm = _cast_weight_t(w_in_t, l, 0, o_gt)
        u_split = _split_even_odd(u, bsz, seq_len).reshape(m, d_model)
        hy = _matmul(u_split, w_hm, F32, "in_proj_hyena", 0, n_hy).reshape(bsz, seq_len, n_hy)
        ml = _matmul(u, w_hm, F32, "in_proj_mlstm", o_ml, n_ml).reshape(bsz, seq_len, n_ml)
        gt = _matmul(u, _cast_weight_t(w_in_t, l, o_gt, n_gt), F32, "in_proj_mlstm_gates")
        w_tail = _cast_weight_t(w_in_t, l, o_rg, n_rg + N_BRANCH * d_model)
        rg = _matmul(u, w_tail, F32, "in_proj_rglru", 0, n_rg)
        gate = _matmul_sigmoid(u, w_tail, b_gate[l].reshape(1, N_BRANCH * d_model).astype(F32),
                               "in_proj_merge_gates", n_rg)

        w1_pad = jnp.pad(hy_w1[l].astype(F32), ((0, LANE - HY_EMB), (0, 0)))
        hdn = _filter_mlp(feat_pad, w1_pad, hy_b1[l].astype(F32), hy_w2[l].astype(F32),
                          hy_b2[l].astype(F32), hy_freq[l].astype(F32))
        k_spec, k_mid = _filter_spectra(hdn, hy_w3[l].astype(F32), t_col, deltas, twc, tws, amat, bmat)
        y_a = _hyena(hy, hy_conv_w[l].astype(F32), hy_conv_b[l].reshape(1, n_hy).astype(F32),
                     hy_skip[l].astype(F32), k_spec, k_mid, twc, tws, amat, bmat)
        y_a = _interleave_even_odd(y_a.reshape(bsz, 2, half, d_mix))

        mixer_steps = bsz * ML_HEADS
        y_b, w_g = _mlstm(ml, 0, gt.reshape(bsz, seq_len, n_gt), ml_gate_b[l], ml_norm[l],
                          _cast_job(w_gate, l, 1, 2 * LANE, ffn_pad, mixer_steps))
        y_c, w_u = _rglru(rg.reshape(bsz, seq_len, n_rg), rg_conv_w[l], rg_conv_b[l],
                          rg_wa[l], rg_ba[l], rg_wx[l], rg_bx[l], rg_lambda[l],
                          _cast_job(w_up, l, 1, 2 * LANE, ffn_pad, mixer_steps))

        merged = _merge(y_a.reshape(m, d_mix), y_b.reshape(m, d_mix), y_c.reshape(m, d_mix),
                        _cast_weight(w_br_a, l), _cast_weight(w_br_b, l), _cast_weight(w_br_c, l), gate)
        h = _matmul_residual_fullk(merged, _cast_weight(w_out, l), h, "out_proj")

        u = _rmsnorm(h, ffn_norm[l], BF16)
        gu_steps = (m // min(MM_TILE, m)) * (ffn_pad // (MM_TILE // 2))
        ff, w_d = _ffn_gate_up(u, w_g, w_u, _cast_job(w_down, l, 0, LANE, ffn_pad, gu_steps))
        h = _matmul_residual(ff, w_d, h, ffn_pad // 4, "ffn_down")

    out = _rmsnorm(h, final_norm, x.dtype)
    return out.reshape(bsz, seq_len, d_model)
```
